```python
import jax, jax.numpy as jnp
from jax import lax
import numpy as np

D_MODEL = 1024
BATCH = 8
SEQ = 4096
DEPTH = 1

PLE_DIM = 256
DN_HEADS = 8
DN_HEAD_K = 128
DN_HEAD_V = 128
DN_QK = DN_HEADS * DN_HEAD_K
DN_V = DN_HEADS * DN_HEAD_V
DN_CONV = 4
DN_CHUNK = 64
CF_CH = 1024
CF_CONV = 31
N_BRANCH = 2
IN_SPLITS = (2 * DN_QK + DN_V, DN_V, DN_HEADS, DN_HEADS, 2 * CF_CH, CF_CH, N_BRANCH * D_MODEL)
N_IN = 2 * DN_QK + 2 * DN_V + 2 * DN_HEADS + 3 * CF_CH + N_BRANCH * D_MODEL
EPS = 1e-6

kernel_name = "hybrid_gated_deltanet_conformer_block"


def _rmsnorm(x, g):
    xf = x.astype(jnp.float32)
    y = xf * lax.rsqrt(jnp.mean(xf * xf, axis=-1, keepdims=True) + EPS)
    return (y * g.astype(jnp.float32)).astype(x.dtype)


def _layernorm(x, g, b):
    xf = x.astype(jnp.float32)
    xc = xf - jnp.mean(xf, axis=-1, keepdims=True)
    y = xc * lax.rsqrt(jnp.mean(xc * xc, axis=-1, keepdims=True) + EPS)
    return (y * g.astype(jnp.float32) + b.astype(jnp.float32)).astype(x.dtype)


def _l2norm(x):
    xf = x.astype(jnp.float32)
    return xf * lax.rsqrt(jnp.sum(xf * xf, axis=-1, keepdims=True) + EPS)


def _causal_dwconv(x, w):
    width = w.shape[0]
    return lax.conv_general_dilated(
        x, w[:, None, :].astype(x.dtype), window_strides=(1,), padding=[(width - 1, 0)],
        dimension_numbers=("NWC", "WIO", "NWC"), feature_group_count=x.shape[-1])


def _chunk_gated_delta_rule(q, k, v, log_decay, beta):
    b, t, h, dk = q.shape
    dv = v.shape[-1]
    c = DN_CHUNK
    n = t // c

    def chunks(a):
        return a.astype(jnp.float32).reshape(b, n, c, h, -1).transpose(0, 3, 1, 2, 4)

    q = chunks(q) * (dk ** -0.5)
    k = chunks(k)
    v = chunks(v)
    g = jnp.cumsum(chunks(log_decay[..., None])[..., 0], axis=-1)
    bt = chunks(beta[..., None])[..., 0]
    incl = jnp.tril(jnp.ones((c, c), dtype=bool))
    strict = jnp.tril(jnp.ones((c, c), dtype=bool), -1)
    decay = jnp.where(incl, jnp.exp(jnp.where(incl, g[..., :, None] - g[..., None, :], 0.0)), 0.0)
    kb = k * bt[..., None]
    a = jnp.where(strict, jnp.einsum("bhncd,bhnsd->bhncs", kb, k) * decay, 0.0)
    eye = jnp.eye(c, dtype=jnp.float32)
    rhs = jnp.concatenate([v * bt[..., None], kb * jnp.exp(g)[..., None]], axis=-1)
    sol = lax.linalg.triangular_solve(a + eye, rhs, left_side=True, lower=True, unit_diagonal=True)
    u, w = sol[..., :dv], sol[..., dv:]
    qk = jnp.where(incl, jnp.einsum("bhncd,bhnsd->bhncs", q, k) * decay, 0.0)
    g_last = g[..., -1:]
    q_g = q * jnp.exp(g)[..., None]
    k_d = k * jnp.exp(g_last - g)[..., None]
    xs = (jnp.moveaxis(q_g, 2, 0), jnp.moveaxis(k_d, 2, 0), jnp.moveaxis(u, 2, 0),
          jnp.moveaxis(w, 2, 0), jnp.moveaxis(qk, 2, 0), jnp.moveaxis(jnp.exp(g_last), 2, 0))

    def step(state, inp):
        q_i, k_i, u_i, w_i, qk_i, gl_i = inp
        v_new = u_i - jnp.einsum("bhcd,bhde->bhce", w_i, state)
        o_i = jnp.einsum("bhcd,bhde->bhce", q_i, state) + jnp.einsum("bhcs,bhse->bhce", qk_i, v_new)
        state = state * gl_i[..., None] + jnp.einsum("bhcd,bhce->bhde", k_i, v_new)
        return state, o_i

    s0 = jnp.zeros((b, h, dk, dv), jnp.float32)
    _, o = lax.scan(step, s0, xs)
    return o.transpose(1, 0, 3, 2, 4).reshape(b, t, h, dv)


def setup_inputs(seed: int = 0) -> dict:
    key = jax.random.key(seed)
    ks = jax.random.split(key, 20)
    f32 = jnp.float32

    def nrm(k, shape, scale):
        return jax.random.normal(k, shape, f32) * scale

    dt = jnp.exp(jax.random.uniform(ks[5], (DEPTH, DN_HEADS), f32, np.log(1e-3), np.log(1e-1)))
    return {
        "x": nrm(ks[0], (BATCH, SEQ, D_MODEL), 1.0),
        "p": nrm(ks[1], (DEPTH, BATCH, SEQ, PLE_DIM), 1.0),
        "mix_norm_g": 1.0 + nrm(ks[2], (DEPTH, D_MODEL), 0.02),
        "w_in": nrm(ks[3], (DEPTH, D_MODEL, N_IN), D_MODEL ** -0.5),
        "dn_conv_w": nrm(ks[4], (DEPTH, DN_CONV, 2 * DN_QK + DN_V), DN_CONV ** -0.5),
        "dn_a_log": jnp.log(jax.random.uniform(ks[6], (DEPTH, DN_HEADS), f32, 1.0, 16.0)),
        "dn_dt_bias": dt + jnp.log(-jnp.expm1(-dt)),
        "dn_out_norm_g": 1.0 + nrm(ks[7], (DEPTH, DN_HEAD_V), 0.02),
        "w_dn_out": nrm(ks[8], (DEPTH, DN_V, D_MODEL), DN_V ** -0.5),
        "cf_dw_w": nrm(ks[9], (DEPTH, CF_CONV, CF_CH), CF_CONV ** -0.5),
        "cf_dw_b": nrm(ks[10], (DEPTH, CF_CH), 0.02),
        "cf_ln_g": 1.0 + nrm(ks[11], (DEPTH, CF_CH), 0.02),
        "cf_ln_b": nrm(ks[12], (DEPTH, CF_CH), 0.02),
        "w_cf_out": nrm(ks[13], (DEPTH, CF_CH, D_MODEL), CF_CH ** -0.5),
        "w_out": nrm(ks[14], (DEPTH, D_MODEL, D_MODEL), D_MODEL ** -0.5),
        "ple_norm_g": 1.0 + nrm(ks[15], (DEPTH, D_MODEL), 0.02),
        "w_ple_gate": nrm(ks[16], (DEPTH, D_MODEL, D_MODEL), D_MODEL ** -0.5),
        "w_ple_proj": nrm(ks[17], (DEPTH, PLE_DIM, D_MODEL), PLE_DIM ** -0.5),
        "final_norm_g": 1.0 + nrm(ks[18], (D_MODEL,), 0.02),
    }


def reference(x, p, mix_norm_g, w_in, dn_conv_w, dn_a_log, dn_dt_bias, dn_out_norm_g, w_dn_out,
              cf_dw_w, cf_dw_b, cf_ln_g, cf_ln_b, w_cf_out, w_out, ple_norm_g, w_ple_gate,
              w_ple_proj, final_norm_g):
    b, t = x.shape[0], x.shape[1]
    split_points = np.cumsum(np.array(IN_SPLITS))[:-1].tolist()
    for i in range(DEPTH):
        h = _rmsnorm(x, mix_norm_g[i])
        proj = jnp.einsum("btd,dn->btn", h, w_in[i])
        qkv, z_dn, beta_l, decay_l, glu, z_cf, gate_l = jnp.split(proj, split_points, axis=-1)

        qkv = jax.nn.silu(_causal_dwconv(qkv, dn_conv_w[i]))
        q, k, v = jnp.split(qkv, [DN_QK, 2 * DN_QK], axis=-1)
        q = _l2norm(q.reshape(b, t, DN_HEADS, DN_HEAD_K))
        k = _l2norm(k.reshape(b, t, DN_HEADS, DN_HEAD_K))
        v = v.reshape(b, t, DN_HEADS, DN_HEAD_V)
        beta = jax.nn.sigmoid(beta_l.astype(jnp.float32))
        log_decay = -jnp.exp(dn_a_log[i].astype(jnp.float32)) * jax.nn.softplus(
            decay_l.astype(jnp.float32) + dn_dt_bias[i].astype(jnp.float32))
        o = _chunk_gated_delta_rule(q, k, v, log_decay, beta)
        o = _rmsnorm(o, dn_out_norm_g[i]).astype(x.dtype) * jax.nn.silu(
            z_dn.reshape(b, t, DN_HEADS, DN_HEAD_V))
        u_dn = jnp.einsum("btc,cd->btd", o.reshape(b, t, DN_V), w_dn_out[i])

        c = glu[..., :CF_CH] * jax.nn.sigmoid(glu[..., CF_CH:])
        c = _causal_dwconv(c, cf_dw_w[i]) + cf_dw_b[i]
        c = jax.nn.silu(_layernorm(c, cf_ln_g[i], cf_ln_b[i])) * jax.nn.silu(z_cf)
        u_cf = jnp.einsum("btc,cd->btd", c, w_cf_out[i])

        g_dn, g_cf = jnp.split(jax.nn.sigmoid(gate_l), N_BRANCH, axis=-1)
        x = x + jnp.einsum("btd,de->bte", g_dn * u_dn + g_cf * u_cf, w_out[i])

        e = jnp.einsum("btp,pd->btd", p[i], w_ple_proj[i])
        gate = jax.nn.sigmoid(jnp.einsum("btd,de->bte", _rmsnorm(x, ple_norm_g[i]), w_ple_gate[i]))
        x = x + gate * e
    return _rmsnorm(x, final_norm_g)
```

```python
import functools

import jax
import jax.numpy as jnp
from jax import lax
from jax.experimental import pallas as pl
from jax.experimental.pallas import tpu as pltpu

F32 = jnp.float32
BF16 = jnp.bfloat16

EPS = 1e-6
N_HEADS = 8
HEAD_DIM = 128
DN_WIDTH = N_HEADS * HEAD_DIM
DN_CONV = 4
CHUNK = 64
CF_CONV = 31
LANES = 128
SUBLANES = 8
GROUP = 2 * LANES
N_GROUPS = 3 * DN_WIDTH // GROUP
CF_HIST = 32
VMEM_LIMIT = 56 * 1024 * 1024


def _sigmoid(x):
    return 1.0 / (1.0 + jnp.exp(-x))


def _silu(x):
    return x * _sigmoid(x)


def _rms_scale(x):
    return x * lax.rsqrt(jnp.mean(x * x, axis=-1, keepdims=True) + EPS)


def _dot(a, b):
    return jnp.dot(a, b, preferred_element_type=F32)


def _dot_nt(a, b):
    return lax.dot_general(a, b, (((1,), (1,)), ((), ())), preferred_element_type=F32)


def _dot_tn(a, b):
    return lax.dot_general(a, b, (((0,), (0,)), ((), ())), preferred_element_type=F32)


def _const_spec(shape):
    nd = len(shape)
    return pl.BlockSpec(shape, lambda b, t: (0,) * nd, pipeline_mode=pl.Buffered(1))


def _dn_proj_kernel(x_ref, g_ref, wqkv_ref, wbd_ref, cw_ref, bdp_ref, ltri_ref,
                    qkv_ref, bg_ref, h_ref, pbuf_ref, tail_ref, *, tm, rows):
    t = pl.program_id(1)
    h_ref[...] = (_rms_scale(x_ref[0]) * g_ref[...]).astype(BF16)

    @pl.when(t == 0)
    def _():
        tail_ref[...] = jnp.zeros_like(tail_ref)

    def group(c, carry):
        pre = _dot(h_ref[...], wqkv_ref[c])
        pbuf_ref[0:SUBLANES, :] = tail_ref[c]
        pbuf_ref[SUBLANES:, :] = pre
        tail_ref[c] = pbuf_ref[tm:tm + SUBLANES, :]
        cw = cw_ref[c]
        is_qk = c < 2 * N_GROUPS // 3
        qscale = jnp.where(c < N_GROUPS // 3, HEAD_DIM ** -0.5, 1.0).astype(F32)
        for r in range(tm // rows):
            base = r * rows + SUBLANES - (DN_CONV - 1)
            acc = None
            for j in range(DN_CONV):
                term = pbuf_ref[base + j:base + j + rows, :] * cw[j:j + 1, :]
                acc = term if acc is None else acc + term
            y = _silu(acc)
            halves = []
            for s in range(GROUP // LANES):
                yh = y[:, s * LANES:(s + 1) * LANES]
                inv = lax.rsqrt(jnp.sum(yh * yh, axis=-1, keepdims=True) + EPS) * qscale
                halves.append(yh * jnp.where(is_qk, inv, 1.0))
            qkv_ref[0, c, r * rows:(r + 1) * rows, :] = jnp.concatenate(halves, axis=1).astype(BF16)
        return carry

    lax.fori_loop(0, N_GROUPS, group, 0)

    logits = _dot(h_ref[...], wbd_ref[...])
    lane = lax.broadcasted_iota(jnp.int32, (CHUNK, LANES), 1)
    z = logits + bdp_ref[1:2, :]
    softplus = jnp.maximum(z, 0.0) + jnp.log1p(jnp.exp(-jnp.abs(z)))
    log_decay = -jnp.exp(bdp_ref[0:1, :]) * softplus
    beta = _sigmoid(logits)
    ltri = ltri_ref[...]
    for ch in range(tm // CHUNK):
        sl = slice(ch * CHUNK, (ch + 1) * CHUNK)
        seg = jnp.where(lane >= N_HEADS, log_decay[sl, :], 0.0)
        hi = seg.astype(BF16)
        r1 = seg - hi.astype(F32)
        mid = r1.astype(BF16)
        lo = (r1 - mid.astype(F32)).astype(BF16)
        g = _dot(ltri, hi) + _dot(ltri, mid) + _dot(ltri, lo)
        bg_ref[0, sl, :] = jnp.where(lane < N_HEADS, beta[sl, :], g)


def _dn_proj(x, norm_g, wqkv, wbd, cw, bdp, *, tm):
    b, t, d = x.shape
    rows = min(tm, 64)
    ltri = jnp.tril(jnp.ones((CHUNK, CHUNK), F32)).astype(BF16)
    kernel = functools.partial(_dn_proj_kernel, tm=tm, rows=rows)
    return pl.pallas_call(
        kernel,
        grid=(b, t // tm),
        in_specs=[
            pl.BlockSpec((1, tm, d), lambda i, j: (i, j, 0)),
            _const_spec((1, d)),
            _const_spec((N_GROUPS, d, GROUP)),
            _const_spec((d, LANES)),
            _const_spec((N_GROUPS, SUBLANES, GROUP)),
            _const_spec((SUBLANES, LANES)),
            _const_spec((CHUNK, CHUNK)),
        ],
        out_specs=[
            pl.BlockSpec((1, N_GROUPS, tm, GROUP), lambda i, j: (i, 0, j, 0)),
            pl.BlockSpec((1, tm, LANES), lambda i, j: (i, j, 0)),
        ],
        out_shape=[
            jax.ShapeDtypeStruct((b, N_GROUPS, t, GROUP), BF16),
            jax.ShapeDtypeStruct((b, t, LANES), F32),
        ],
        scratch_shapes=[
            pltpu.VMEM((tm, d), BF16),
            pltpu.VMEM((tm + SUBLANES, GROUP), F32),
            pltpu.VMEM((N_GROUPS, SUBLANES, GROUP), F32),
        ],
        compiler_params=pltpu.CompilerParams(
            dimension_semantics=("arbitrary", "arbitrary"), vmem_limit_bytes=VMEM_LIMIT),
        name="dn_proj",
    )(x, norm_g, wqkv, wbd, cw, bdp, ltri)


def _dn_delta_kernel(qkv_ref, bg_ref, o_ref, state_ref, *, tb):
    @pl.when(pl.program_id(1) == 0)
    def _():
        state_ref[...] = jnp.zeros_like(state_ref)

    row = lax.broadcasted_iota(jnp.int32, (CHUNK, CHUNK), 0)
    col = lax.broadcasted_iota(jnp.int32, (CHUNK, CHUNK), 1)
    incl = row >= col
    strict = row > col
    n_levels = CHUNK.bit_length() - 2

    def chunk(ci, carry):
        r0 = pl.multiple_of(ci * CHUNK, CHUNK)
        bg = bg_ref[0, pl.ds(r0, CHUNK), :]
        bg_t = bg.T
        for h in range(N_HEADS):
            grp, lo = h // 2, (h % 2) * LANES
            q = qkv_ref[0, grp, pl.ds(r0, CHUNK), lo:lo + LANES]
            k = qkv_ref[0, N_GROUPS // 3 + grp, pl.ds(r0, CHUNK), lo:lo + LANES]
            v = qkv_ref[0, 2 * N_GROUPS // 3 + grp, pl.ds(r0, CHUNK), lo:lo + LANES]
            beta = bg[:, h:h + 1]
            g_col = bg[:, N_HEADS + h:N_HEADS + h + 1]
            g_row = bg_t[N_HEADS + h:N_HEADS + h + 1, :]
            g_last = g_col[CHUNK - 1:CHUNK, :]
            decay = jnp.where(incl, jnp.exp(jnp.where(incl, g_col - g_row, 0.0)), 0.0)
            kf = k.astype(F32)
            kb = kf * beta
            eg = jnp.exp(g_col)
            kq = _dot_nt(jnp.concatenate([kb.astype(BF16), q], axis=0), k)
            a = jnp.where(strict, kq[:CHUNK] * decay, 0.0)
            qk = jnp.where(incl, kq[CHUNK:] * decay, 0.0)
            n = -a
            pw = _dot(a.astype(BF16), a.astype(BF16))
            for level in range(n_levels):
                pwb = pw.astype(BF16)
                if level + 1 < n_levels:
                    both = _dot(jnp.concatenate([n, pw], axis=0).astype(BF16), pwb)
                    n = n + pw + both[:CHUNK]
                    pw = both[CHUNK:]
                else:
                    n = n + pw + _dot(n.astype(BF16), pwb)
            rhs = jnp.concatenate([v.astype(F32) * beta, kb * eg], axis=1)
            sol = rhs + _dot(n.astype(BF16), rhs.astype(BF16))
            u, w = sol[:, :LANES], sol[:, LANES:]
            q_g = q.astype(F32) * eg
            k_d = kf * jnp.exp(g_last - g_col)
            state = state_ref[h]
            sb = state.astype(BF16)
            ws = _dot(jnp.concatenate([w, q_g], axis=0).astype(BF16), sb)
            v_new = u - ws[:CHUNK]
            vb = v_new.astype(BF16)
            o = ws[CHUNK:] + _dot(qk.astype(BF16), vb)
            state_ref[h] = state * jnp.exp(g_last) + _dot_tn(k_d.astype(BF16), vb)
            o_ref[0, pl.ds(r0, CHUNK), h * LANES:(h + 1) * LANES] = o.astype(o_ref.dtype)
        return carry

    lax.fori_loop(0, tb // CHUNK, chunk, 0)


def _dn_delta(qkv, bg, *, tb):
    b, _, t, _ = qkv.shape
    return pl.pallas_call(
        functools.partial(_dn_delta_kernel, tb=tb),
        grid=(b, t // tb),
        in_specs=[
            pl.BlockSpec((1, N_GROUPS, tb, GROUP), lambda i, j: (i, 0, j, 0)),
            pl.BlockSpec((1, tb, LANES), lambda i, j: (i, j, 0)),
        ],
        out_specs=pl.BlockSpec((1, tb, DN_WIDTH), lambda i, j: (i, j, 0)),
        out_shape=jax.ShapeDtypeStruct((b, t, DN_WIDTH), BF16),
        scratch_shapes=[pltpu.VMEM((N_HEADS, HEAD_DIM, HEAD_DIM), F32)],
        compiler_params=pltpu.CompilerParams(
            dimension_semantics=("arbitrary", "arbitrary"), vmem_limit_bytes=VMEM_LIMIT),
        name="dn_delta",
    )(qkv, bg)


def _mix_out_kernel(x_ref, o_ref, p_ref, w3_ref, wdn_ref, wcf_ref, wout_ref, wpg_ref, wpp_ref,
                    vec_ref, dww_ref, out_ref, cbuf_ref, conv_ref, *, tm, rows, final):
    t = pl.program_id(1)
    x = x_ref[0]
    mix_g, dn_g, dw_b, ln_g, ln_b, ple_g, fin_g = (vec_ref[i:i + 1, :] for i in range(7))
    h = (_rms_scale(x) * mix_g).astype(BF16)

    o = o_ref[0].astype(F32)
    normed = [_rms_scale(o[:, i * HEAD_DIM:(i + 1) * HEAD_DIM]) for i in range(N_HEADS)]
    o_n = (jnp.concatenate(normed, axis=1) * dn_g).astype(x.dtype)
    a_in = o_n * _silu(_dot(h, w3_ref[0]))
    u_dn = _dot(a_in.astype(BF16), wdn_ref[...])

    c = _dot(h, w3_ref[1]) * _sigmoid(_dot(h, w3_ref[2]))

    @pl.when(t == 0)
    def _():
        cbuf_ref[0:CF_HIST, :] = jnp.zeros((CF_HIST, cbuf_ref.shape[1]), F32)

    cbuf_ref[CF_HIST:, :] = c
    first = CF_HIST - (CF_CONV - 1)
    n_col = cbuf_ref.shape[1] // LANES
    for r in range(tm // rows):
        for cg in range(n_col):
            cs = slice(cg * LANES, (cg + 1) * LANES)
            acc = None
            for s in range(SUBLANES):
                taps = [j for j in range(CF_CONV) if (first + j) % SUBLANES == s]
                if not taps:
                    continue
                lo = first + taps[0]
                span = taps[-1] - taps[0] + rows
                win = cbuf_ref[r * rows + lo:r * rows + lo + span, cs]
                for j in taps:
                    off = j - taps[0]
                    term = win[off:off + rows, :] * dww_ref[j:j + 1, cs]
                    acc = term if acc is None else acc + term
            conv_ref[r * rows:(r + 1) * rows, cs] = acc
    cbuf_ref[0:CF_HIST, :] = cbuf_ref[tm:tm + CF_HIST, :]

    cv = conv_ref[...] + dw_b
    cc = cv - jnp.mean(cv, axis=-1, keepdims=True)
    ln = cc * lax.rsqrt(jnp.mean(cc * cc, axis=-1, keepdims=True) + EPS) * ln_g + ln_b
    b_in = _silu(ln) * _silu(_dot(h, w3_ref[3]))
    u_cf = _dot(b_in.astype(BF16), wcf_ref[...])

    merged = _sigmoid(_dot(h, w3_ref[4])) * u_dn + _sigmoid(_dot(h, w3_ref[5])) * u_cf
    x1 = x + _dot(merged.astype(BF16), wout_ref[...])

    e = _dot(p_ref[0].astype(BF16), wpp_ref[...])
    gate = _sigmoid(_dot((_rms_scale(x1) * ple_g).astype(BF16), wpg_ref[...]))
    x2 = x1 + gate * e
    if final:
        x2 = _rms_scale(x2) * fin_g
    out_ref[0] = x2


def _mix_out(x, o, p, w3, wdn, wcf, wout, wpg, wpp, vecs, dww, *, tm, final):
    b, t, d = x.shape
    cf = wcf.shape[0]
    rows = min(tm, 64)
    kernel = functools.partial(_mix_out_kernel, tm=tm, rows=rows, final=final)
    tile = lambda w: pl.BlockSpec((1, tm, w), lambda i, j: (i, j, 0))
    return pl.pallas_call(
        kernel,
        grid=(b, t // tm),
        in_specs=[
            tile(d), tile(o.shape[-1]), tile(p.shape[-1]),
            _const_spec(w3.shape), _const_spec(wdn.shape), _const_spec(wcf.shape),
            _const_spec(wout.shape), _const_spec(wpg.shape), _const_spec(wpp.shape),
            _const_spec(vecs.shape), _const_spec(dww.shape),
        ],
        out_specs=tile(d),
        out_shape=jax.ShapeDtypeStruct((b, t, d), x.dtype),
        scratch_shapes=[
            pltpu.VMEM((tm + CF_HIST, cf), F32),
            pltpu.VMEM((tm, cf), F32),
        ],
        compiler_params=pltpu.CompilerParams(
            dimension_semantics=("arbitrary", "arbitrary"), vmem_limit_bytes=VMEM_LIMIT),
        name="mix_out",
    )(x, o, p, w3, wdn, wcf, wout, wpg, wpp, vecs, dww)


def _pad_rows(a, n):
    return jnp.pad(a, ((0, n - a.shape[0]), (0, 0)))


def _forward(x, p, mix_norm_g, w_in, dn_conv_w, dn_a_log, dn_dt_bias, dn_out_norm_g, w_dn_out,
             cf_dw_w, cf_dw_b, cf_ln_g, cf_ln_b, w_cf_out, w_out, ple_norm_g, w_ple_gate,
             w_ple_proj, final_norm_g, *, tm1, tb, tm3):
    depth, d, _ = w_in.shape
    cf = w_cf_out.shape[1]
    qkv_w = 3 * DN_WIDTH
    o_z, o_bd = qkv_w, qkv_w + DN_WIDTH
    o_glu = o_bd + 2 * N_HEADS
    o_zcf = o_glu + 2 * cf
    o_gate = o_zcf + cf
    row = lambda v: v.reshape(1, -1).astype(F32)
    for i in range(depth):
        wi = w_in[i]
        wqkv = wi[:, :qkv_w].reshape(d, N_GROUPS, GROUP).transpose(1, 0, 2).astype(BF16)
        wbd = jnp.pad(wi[:, o_bd:o_glu], ((0, 0), (0, LANES - 2 * N_HEADS))).astype(BF16)
        cw = _pad_rows(dn_conv_w[i].astype(F32), SUBLANES)
        cw = cw.reshape(SUBLANES, N_GROUPS, GROUP).transpose(1, 0, 2)
        pad_heads = lambda v: jnp.pad(v.astype(F32), (N_HEADS, LANES - 2 * N_HEADS)).reshape(1, LANES)
        bdp = _pad_rows(jnp.concatenate([pad_heads(dn_a_log[i]), pad_heads(dn_dt_bias[i])]), SUBLANES)
        w3 = jnp.stack([wi[:, o_z:o_bd], wi[:, o_glu:o_glu + cf], wi[:, o_glu + cf:o_zcf],
                        wi[:, o_zcf:o_gate], wi[:, o_gate:o_gate + d], wi[:, o_gate + d:]]).astype(BF16)
        vecs = _pad_rows(jnp.concatenate([
            row(mix_norm_g[i]), row(jnp.tile(dn_out_norm_g[i], N_HEADS)), row(cf_dw_b[i]),
            row(cf_ln_g[i]), row(cf_ln_b[i]), row(ple_norm_g[i]), row(final_norm_g)]), SUBLANES)
        dww = _pad_rows(cf_dw_w[i].astype(F32), CF_HIST)

        qkv, bg = _dn_proj(x, row(mix_norm_g[i]), wqkv, wbd, cw, bdp, tm=tm1)
        o = _dn_delta(qkv, bg, tb=tb)
        x = _mix_out(x, o, p[i], w3, w_dn_out[i].astype(BF16), w_cf_out[i].astype(BF16),
                     w_out[i].astype(BF16), w_ple_gate[i].astype(BF16), w_ple_proj[i].astype(BF16),
                     vecs, dww, tm=tm3, final=(i == depth - 1))
    return x


def kernel(x, p, mix_norm_g, w_in, dn_conv_w, dn_a_log, dn_dt_bias, dn_out_norm_g, w_dn_out,
           cf_dw_w, cf_dw_b, cf_ln_g, cf_ln_b, w_cf_out, w_out, ple_norm_g, w_ple_gate,
           w_ple_proj, final_norm_g):
    t = x.shape[1]
    return _forward(x, p, mix_norm_g, w_in, dn_conv_w, dn_a_log, dn_dt_bias, dn_out_norm_g,
                    w_dn_out, cf_dw_w, cf_dw_b, cf_ln_g, cf_ln_b, w_cf_out, w_out, ple_norm_g,
                    w_ple_gate, w_ple_proj, final_norm_g,
                    tm1=min(t, 512), tb=min(t, 512), tm3=min(t, 256))
```

```python
import functools

import jax
import jax.numpy as jnp
from jax import lax
from jax.experimental import pallas as pl
from jax.experimental.pallas import tpu as pltpu

F32 = jnp.float32
BF16 = jnp.bfloat16

EPS = 1e-6
N_HEADS = 8
HEAD_DIM = 128
DN_WIDTH = N_HEADS * HEAD_DIM
DN_CONV = 4
CHUNK = 64
CF_CONV = 31
LANES = 128
SUBLANES = 8
GROUP = 2 * LANES
N_GROUPS = 3 * DN_WIDTH // GROUP
CF_HIST = 32
VMEM_LIMIT = 56 * 1024 * 1024


def _sigmoid(x):
    return 1.0 / (1.0 + jnp.exp(-x))


def _silu(x):
    return x * _sigmoid(x)


def _rms_scale(x):
    return x * lax.rsqrt(jnp.mean(x * x, axis=-1, keepdims=True) + EPS)


def _dot(a, b):
    return jnp.dot(a, b, preferred_element_type=F32)


def _dot_nt(a, b):
    return lax.dot_general(a, b, (((1,), (1,)), ((), ())), preferred_element_type=F32)


def _bdot(a, b):
    return lax.dot_general(a, b, (((2,), (1,)), ((0,), (0,))), preferred_element_type=F32)


def _bdot_nt(a, b):
    return lax.dot_general(a, b, (((2,), (2,)), ((0,), (0,))), preferred_element_type=F32)


def _const_spec(shape):
    nd = len(shape)
    return pl.BlockSpec(shape, lambda b, t: (0,) * nd, pipeline_mode=pl.Buffered(1))


def _dn_proj_kernel(x_ref, g_ref, wqkv_ref, wbd_ref, cw_ref, bdp_ref, ltri_ref,
                    qkv_ref, bg_ref, h_ref, pbuf_ref, tail_ref, *, tm, rows):
    t = pl.program_id(1)
    h_ref[...] = (_rms_scale(x_ref[0]) * g_ref[...]).astype(BF16)

    @pl.when(t == 0)
    def _():
        tail_ref[...] = jnp.zeros_like(tail_ref)

    def group(c, carry):
        pre = _dot(h_ref[...], wqkv_ref[c])
        pbuf_ref[0:SUBLANES, :] = tail_ref[c]
        pbuf_ref[SUBLANES:, :] = pre
        tail_ref[c] = pbuf_ref[tm:tm + SUBLANES, :]
        cw = cw_ref[c]
        is_qk = c < 2 * N_GROUPS // 3
        qscale = jnp.where(c < N_GROUPS // 3, HEAD_DIM ** -0.5, 1.0).astype(F32)
        for r in range(tm // rows):
            base = r * rows + SUBLANES - (DN_CONV - 1)
            acc = None
            for j in range(DN_CONV):
                term = pbuf_ref[base + j:base + j + rows, :] * cw[j:j + 1, :]
                acc = term if acc is None else acc + term
            y = _silu(acc)
            halves = []
            for s in range(GROUP // LANES):
                yh = y[:, s * LANES:(s + 1) * LANES]
                inv = lax.rsqrt(jnp.sum(yh * yh, axis=-1, keepdims=True) + EPS) * qscale
                halves.append(yh * jnp.where(is_qk, inv, 1.0))
            qkv_ref[0, c, r * rows:(r + 1) * rows, :] = jnp.concatenate(halves, axis=1).astype(BF16)
        return carry

    lax.fori_loop(0, N_GROUPS, group, 0)

    logits = _dot(h_ref[...], wbd_ref[...])
    lane = lax.broadcasted_iota(jnp.int32, (CHUNK, LANES), 1)
    z = logits + bdp_ref[1:2, :]
    softplus = jnp.maximum(z, 0.0) + jnp.log1p(jnp.exp(-jnp.abs(z)))
    log_decay = -jnp.exp(bdp_ref[0:1, :]) * softplus
    beta = _sigmoid(logits)
    ltri = ltri_ref[...]
    for ch in range(tm // CHUNK):
        sl = slice(ch * CHUNK, (ch + 1) * CHUNK)
        seg = jnp.where(lane >= N_HEADS, log_decay[sl, :], 0.0)
        hi = seg.astype(BF16)
        r1 = seg - hi.astype(F32)
        mid = r1.astype(BF16)
        lo = (r1 - mid.astype(F32)).astype(BF16)
        g = _dot(ltri, hi) + _dot(ltri, mid) + _dot(ltri, lo)
        bg_ref[0, sl, :] = jnp.where(lane < N_HEADS, beta[sl, :], g)


def _dn_proj(x, norm_g, wqkv, wbd, cw, bdp, *, tm):
    b, t, d = x.shape
    rows = min(tm, 64)
    ltri = jnp.tril(jnp.ones((CHUNK, CHUNK), F32)).astype(BF16)
    kernel = functools.partial(_dn_proj_kernel, tm=tm, rows=rows)
    return pl.pallas_call(
        kernel,
        grid=(b, t // tm),
        in_specs=[
            pl.BlockSpec((1, tm, d), lambda i, j: (i, j, 0)),
            _const_spec((1, d)),
            _const_spec((N_GROUPS, d, GROUP)),
            _const_spec((d, LANES)),
            _const_spec((N_GROUPS, SUBLANES, GROUP)),
            _const_spec((SUBLANES, LANES)),
            _const_spec((CHUNK, CHUNK)),
        ],
        out_specs=[
            pl.BlockSpec((1, N_GROUPS, tm, GROUP), lambda i, j: (i, 0, j, 0)),
            pl.BlockSpec((1, tm, LANES), lambda i, j: (i, j, 0)),
        ],
        out_shape=[
            jax.ShapeDtypeStruct((b, N_GROUPS, t, GROUP), BF16),
            jax.ShapeDtypeStruct((b, t, LANES), F32),
        ],
        scratch_shapes=[
            pltpu.VMEM((tm, d), BF16),
            pltpu.VMEM((tm + SUBLANES, GROUP), F32),
            pltpu.VMEM((N_GROUPS, SUBLANES, GROUP), F32),
        ],
        compiler_params=pltpu.CompilerParams(
            dimension_semantics=("arbitrary", "arbitrary"), vmem_limit_bytes=VMEM_LIMIT),
        name="dn_proj",
    )(x, norm_g, wqkv, wbd, cw, bdp, ltri)


def _dn_delta_kernel(qkv_ref, bg_ref, o_ref, state_ref, *, tb):
    @pl.when(pl.program_id(1) == 0)
    def _():
        state_ref[...] = jnp.zeros_like(state_ref)

    row = lax.broadcasted_iota(jnp.int32, (CHUNK, CHUNK), 0)
    col = lax.broadcasted_iota(jnp.int32, (CHUNK, CHUNK), 1)
    incl = row >= col
    strict = row > col
    n_levels = CHUNK.bit_length() - 2

    def chunk(ci, carry):
        r0 = pl.multiple_of(ci * CHUNK, CHUNK)
        bg = bg_ref[0, pl.ds(r0, CHUNK), :]
        bg_t = bg.T
        rows = pl.ds(r0, CHUNK)

        def heads(first_group):
            return jnp.stack([qkv_ref[0, first_group + h // 2, rows, (h % 2) * LANES:(h % 2 + 1) * LANES]
                              for h in range(N_HEADS)])

        q, k, v = heads(0), heads(N_GROUPS // 3), heads(2 * N_GROUPS // 3)
        beta = jnp.stack([bg[:, h:h + 1] for h in range(N_HEADS)])
        g_col = jnp.stack([bg[:, N_HEADS + h:N_HEADS + h + 1] for h in range(N_HEADS)])
        g_row = jnp.stack([bg_t[N_HEADS + h:N_HEADS + h + 1, :] for h in range(N_HEADS)])
        g_last = g_col[:, CHUNK - 1:CHUNK, :]
        decay = jnp.where(incl, jnp.exp(jnp.where(incl, g_col - g_row, 0.0)), 0.0)
        kf = k.astype(F32)
        kb = kf * beta
        eg = jnp.exp(g_col)
        kq = _bdot_nt(jnp.concatenate([kb.astype(BF16), q], axis=1), k)
        a = jnp.where(strict, kq[:, :CHUNK] * decay, 0.0)
        qk = jnp.where(incl, kq[:, CHUNK:] * decay, 0.0)
        n = -a
        ab = a.astype(BF16)
        pw = _bdot(ab, ab)
        for level in range(n_levels):
            pwb = pw.astype(BF16)
            if level + 1 < n_levels:
                both = _bdot(jnp.concatenate([n, pw], axis=1).astype(BF16), pwb)
                n = n + pw + both[:, :CHUNK]
                pw = both[:, CHUNK:]
            else:
                n = n + pw + _bdot(n.astype(BF16), pwb)
        rhs = jnp.concatenate([v.astype(F32) * beta, kb * eg], axis=2)
        sol = rhs + _bdot(n.astype(BF16), rhs.astype(BF16))
        u, w = sol[:, :, :HEAD_DIM], sol[:, :, HEAD_DIM:]
        q_g = q.astype(F32) * eg
        k_dt = jnp.swapaxes(kf * jnp.exp(g_last - g_col), 1, 2)
        state = state_ref[...]
        ws = _bdot(jnp.concatenate([w, q_g], axis=1).astype(BF16), state.astype(BF16))
        vb = (u - ws[:, :CHUNK]).astype(BF16)
        ov = _bdot(jnp.concatenate([qk, k_dt], axis=1).astype(BF16), vb)
        state_ref[...] = state * jnp.exp(g_last) + ov[:, CHUNK:]
        o = (ws[:, CHUNK:] + ov[:, :CHUNK]).astype(o_ref.dtype)
        for h in range(N_HEADS):
            o_ref[0, rows, h * LANES:(h + 1) * LANES] = o[h]
        return carry

    lax.fori_loop(0, tb // CHUNK, chunk, 0)


def _dn_delta(qkv, bg, *, tb):
    b, _, t, _ = qkv.shape
    return pl.pallas_call(
        functools.partial(_dn_delta_kernel, tb=tb),
        grid=(b, t // tb),
        in_specs=[
            pl.BlockSpec((1, N_GROUPS, tb, GROUP), lambda i, j: (i, 0, j, 0)),
            pl.BlockSpec((1, tb, LANES), lambda i, j: (i, j, 0)),
        ],
        out_specs=pl.BlockSpec((1, tb, DN_WIDTH), lambda i, j: (i, j, 0)),
        out_shape=jax.ShapeDtypeStruct((b, t, DN_WIDTH), BF16),
        scratch_shapes=[pltpu.VMEM((N_HEADS, HEAD_DIM, HEAD_DIM), F32)],
        compiler_params=pltpu.CompilerParams(
            dimension_semantics=("arbitrary", "arbitrary"), vmem_limit_bytes=VMEM_LIMIT),
        name="dn_delta",
    )(qkv, bg)


def _mix_out_kernel(x_ref, o_ref, p_ref, w3_ref, wdn_ref, wcf_ref, wout_ref, wpg_ref, wpp_ref,
                    vec_ref, dww_ref, out_ref, cbuf_ref, conv_ref, *, tm, rows, final):
    t = pl.program_id(1)
    x = x_ref[0]
    mix_g, dn_g, dw_b, ln_g, ln_b, ple_g, fin_g = (vec_ref[i:i + 1, :] for i in range(7))
    h = (_rms_scale(x) * mix_g).astype(BF16)

    o = o_ref[0].astype(F32)
    normed = [_rms_scale(o[:, i * HEAD_DIM:(i + 1) * HEAD_DIM]) for i in range(N_HEADS)]
    o_n = (jnp.concatenate(normed, axis=1) * dn_g).astype(x.dtype)
    a_in = o_n * _silu(_dot(h, w3_ref[0]))
    u_dn = _dot(a_in.astype(BF16), wdn_ref[...])

    c = _dot(h, w3_ref[1]) * _sigmoid(_dot(h, w3_ref[2]))

    @pl.when(t == 0)
    def _():
        cbuf_ref[0:CF_HIST, :] = jnp.zeros((CF_HIST, cbuf_ref.shape[1]), F32)

    cbuf_ref[CF_HIST:, :] = c
    first = CF_HIST - (CF_CONV - 1)
    n_col = cbuf_ref.shape[1] // LANES
    for r in range(tm // rows):
        for cg in range(n_col):
            cs = slice(cg * LANES, (cg + 1) * LANES)
            acc = None
            for s in range(SUBLANES):
                taps = [j for j in range(CF_CONV) if (first + j) % SUBLANES == s]
                if not taps:
                    continue
                lo = first + taps[0]
                span = taps[-1] - taps[0] + rows
                win = cbuf_ref[r * rows + lo:r * rows + lo + span, cs]
                for j in taps:
                    off = j - taps[0]
                    term = win[off:off + rows, :] * dww_ref[j:j + 1, cs]
                    acc = term if acc is None else acc + term
            conv_ref[r * rows:(r + 1) * rows, cs] = acc
    cbuf_ref[0:CF_HIST, :] = cbuf_ref[tm:tm + CF_HIST, :]

    cv = conv_ref[...] + dw_b
    cc = cv - jnp.mean(cv, axis=-1, keepdims=True)
    ln = cc * lax.rsqrt(jnp.mean(cc * cc, axis=-1, keepdims=True) + EPS) * ln_g + ln_b
    b_in = _silu(ln) * _silu(_dot(h, w3_ref[3]))
    u_cf = _dot(b_in.astype(BF16), wcf_ref[...])

    merged = _sigmoid(_dot(h, w3_ref[4])) * u_dn + _sigmoid(_dot(h, w3_ref[5])) * u_cf
    x1 = x + _dot(merged.astype(BF16), wout_ref[...])

    e = _dot(p_ref[0].astype(BF16), wpp_ref[...])
    gate = _sigmoid(_dot((_rms_scale(x1) * ple_g).astype(BF16), wpg_ref[...]))
    x2 = x1 + gate * e
    if final:
        x2 = _rms_scale(x2) * fin_g
    out_ref[0] = x2


def _mix_out(x, o, p, w3, wdn, wcf, wout, wpg, wpp, vecs, dww, *, tm, final):
    b, t, d = x.shape
    cf = wcf.shape[0]
    rows = min(tm, 64)
    kernel = functools.partial(_mix_out_kernel, tm=tm, rows=rows, final=final)
    tile = lambda w: pl.BlockSpec((1, tm, w), lambda i, j: (i, j, 0))
    return pl.pallas_call(
        kernel,
        grid=(b, t // tm),
        in_specs=[
            tile(d), tile(o.shape[-1]), tile(p.shape[-1]),
            _const_spec(w3.shape), _const_spec(wdn.shape), _const_spec(wcf.shape),
            _const_spec(wout.shape), _const_spec(wpg.shape), _const_spec(wpp.shape),
            _const_spec(vecs.shape), _const_spec(dww.shape),
        ],
        out_specs=tile(d),
        out_shape=jax.ShapeDtypeStruct((b, t, d), x.dtype),
        scratch_shapes=[
            pltpu.VMEM((tm + CF_HIST, cf), F32),
            pltpu.VMEM((tm, cf), F32),
        ],
        compiler_params=pltpu.CompilerParams(
            dimension_semantics=("arbitrary", "arbitrary"), vmem_limit_bytes=VMEM_LIMIT),
        name="mix_out",
    )(x, o, p, w3, wdn, wcf, wout, wpg, wpp, vecs, dww)


def _pad_rows(a, n):
    return jnp.pad(a, ((0, n - a.shape[0]), (0, 0)))


def _forward(x, p, mix_norm_g, w_in, dn_conv_w, dn_a_log, dn_dt_bias, dn_out_norm_g, w_dn_out,
             cf_dw_w, cf_dw_b, cf_ln_g, cf_ln_b, w_cf_out, w_out, ple_norm_g, w_ple_gate,
             w_ple_proj, final_norm_g, *, tm1, tb, tm3):
    depth, d, _ = w_in.shape
    cf = w_cf_out.shape[1]
    qkv_w = 3 * DN_WIDTH
    o_z, o_bd = qkv_w, qkv_w + DN_WIDTH
    o_glu = o_bd + 2 * N_HEADS
    o_zcf = o_glu + 2 * cf
    o_gate = o_zcf + cf
    row = lambda v: v.reshape(1, -1).astype(F32)
    for i in range(depth):
        wi = w_in[i]
        wqkv = wi[:, :qkv_w].reshape(d, N_GROUPS, GROUP).transpose(1, 0, 2).astype(BF16)
        wbd = jnp.pad(wi[:, o_bd:o_glu], ((0, 0), (0, LANES - 2 * N_HEADS))).astype(BF16)
        cw = _pad_rows(dn_conv_w[i].astype(F32), SUBLANES)
        cw = cw.reshape(SUBLANES, N_GROUPS, GROUP).transpose(1, 0, 2)
        pad_heads = lambda v: jnp.pad(v.astype(F32), (N_HEADS, LANES - 2 * N_HEADS)).reshape(1, LANES)
        bdp = _pad_rows(jnp.concatenate([pad_heads(dn_a_log[i]), pad_heads(dn_dt_bias[i])]), SUBLANES)
        w3 = jnp.stack([wi[:, o_z:o_bd], wi[:, o_glu:o_glu + cf], wi[:, o_glu + cf:o_zcf],
                        wi[:, o_zcf:o_gate], wi[:, o_gate:o_gate + d], wi[:, o_gate + d:]]).astype(BF16)
        vecs = _pad_rows(jnp.concatenate([
            row(mix_norm_g[i]), row(jnp.tile(dn_out_norm_g[i], N_HEADS)), row(cf_dw_b[i]),
            row(cf_ln_g[i]), row(cf_ln_b[i]), row(ple_norm_g[i]), row(final_norm_g)]), SUBLANES)
        dww = _pad_rows(cf_dw_w[i].astype(F32), CF_HIST)

        qkv, bg = _dn_proj(x, row(mix_norm_g[i]), wqkv, wbd, cw, bdp, tm=tm1)
        o = _dn_delta(qkv, bg, tb=tb)
        x = _mix_out(x, o, p[i], w3, w_dn_out[i].astype(BF16), w_cf_out[i].astype(BF16),
                     w_out[i].astype(BF16), w_ple_gate[i].astype(BF16), w_ple_proj[i].astype(BF16),
                     vecs, dww, tm=tm3, final=(i == depth - 1))
    return x


def kernel(x, p, mix_norm_g, w_in, dn_conv_w, dn_a_log, dn_dt_bias, dn_out_norm_g, w_dn_out,
           cf_dw_w, cf_dw_b, cf_ln_g, cf_ln_b, w_cf_out, w_out, ple_norm_g, w_ple_gate,
           w_ple_proj, final_norm_g):
    t = x.shape[1]
    return _forward(x, p, mix_norm_g, w_in, dn_conv_w, dn_a_log, dn_dt_bias, dn_out_norm_g,
                    w_dn_out, cf_dw_w, cf_dw_b, cf_ln_g, cf_ln_b, w_cf_out, w_out, ple_norm_g,
                    w_ple_gate, w_ple_proj, final_norm_g,
                    tm1=min(t, 512), tb=min(t, 512), tm3=min(t, 256))
```

```python
import functools

import jax
import jax.numpy as jnp
from jax import lax
from jax.experimental import pallas as pl
from jax.experimental.pallas import tpu as pltpu

F32 = jnp.float32
BF16 = jnp.bfloat16

EPS = 1e-6
N_HEADS = 8
HEAD_DIM = 128
DN_WIDTH = N_HEADS * HEAD_DIM
DN_CONV = 4
CHUNK = 64
CF_CONV = 31
LANES = 128
SUBLANES = 8
CF_HIST = 32
CF_SLABS = SUBLANES
CF_TIME_BLOCK = 8
VMEM_LIMIT = 56 * 1024 * 1024


def _sigmoid(x):
    return 1.0 / (1.0 + jnp.exp(-x))


def _silu(x):
    return x * _sigmoid(x)


def _rms_scale(x):
    return x * lax.rsqrt(jnp.mean(x * x, axis=-1, keepdims=True) + EPS)


def _dot(a, b):
    return jnp.dot(a, b, preferred_element_type=F32)


def _bdot(a, b):
    return lax.dot_general(a, b, (((2,), (1,)), ((0,), (0,))), preferred_element_type=F32)


def _bdot_nt(a, b):
    return lax.dot_general(a, b, (((2,), (2,)), ((0,), (0,))), preferred_element_type=F32)


def _const_spec(shape):
    nd = len(shape)
    return pl.BlockSpec(shape, lambda b, t: (0,) * nd, pipeline_mode=pl.Buffered(1))


def _dn_proj_kernel(x_ref, g_ref, wqkv_ref, wbd_ref, cw_ref, bdp_ref, ltri_ref,
                    qkv_ref, bg_ref, h_ref, pz_ref, yz_ref, tail_ref, *, tm):
    t = pl.program_id(1)
    h_ref[...] = (_rms_scale(x_ref[0]) * g_ref[...]).astype(BF16)

    @pl.when(t == 0)
    def _():
        tail_ref[...] = jnp.zeros_like(tail_ref)

    hist = (DN_CONV - 1) * N_HEADS
    for sec in range(3):
        pre = _dot(h_ref[...], wqkv_ref[sec])
        pz_ref[0:hist, :] = tail_ref[sec]
        for h in range(N_HEADS):
            pz_ref[pl.ds(hist + h, tm, stride=N_HEADS), :] = pre[:, h * HEAD_DIM:(h + 1) * HEAD_DIM]
        tail_ref[sec] = pz_ref[tm * N_HEADS:tm * N_HEADS + hist, :]
        p3 = pz_ref[...].reshape(tm + DN_CONV - 1, N_HEADS, HEAD_DIM)
        acc = None
        for j in range(DN_CONV):
            term = p3[j:j + tm] * cw_ref[sec, j]
            acc = term if acc is None else acc + term
        y = _silu(acc)
        if sec < 2:
            inv = lax.rsqrt(jnp.sum(y * y, axis=-1, keepdims=True) + EPS)
            y = y * (inv * HEAD_DIM ** -0.5 if sec == 0 else inv)
        yz_ref[...] = y.reshape(tm * N_HEADS, HEAD_DIM)
        qkv_ref[0, sec] = jnp.concatenate(
            [yz_ref[pl.ds(h, tm, stride=N_HEADS), :] for h in range(N_HEADS)], axis=1).astype(BF16)

    logits = _dot(h_ref[...], wbd_ref[...])
    lane = lax.broadcasted_iota(jnp.int32, (CHUNK, LANES), 1)
    z = logits + bdp_ref[1:2, :]
    softplus = jnp.maximum(z, 0.0) + jnp.log1p(jnp.exp(-jnp.abs(z)))
    log_decay = -jnp.exp(bdp_ref[0:1, :]) * softplus
    beta = _sigmoid(logits)
    ltri = ltri_ref[...]
    for ch in range(tm // CHUNK):
        sl = slice(ch * CHUNK, (ch + 1) * CHUNK)
        seg = jnp.where(lane >= N_HEADS, log_decay[sl, :], 0.0)
        hi = seg.astype(BF16)
        r1 = seg - hi.astype(F32)
        mid = r1.astype(BF16)
        lo = (r1 - mid.astype(F32)).astype(BF16)
        g = _dot(ltri, hi) + _dot(ltri, mid) + _dot(ltri, lo)
        bg_ref[0, sl, :] = jnp.where(lane < N_HEADS, beta[sl, :], g)


def _dn_proj(x, norm_g, wqkv, wbd, cw, bdp, *, tm):
    b, t, d = x.shape
    ltri = jnp.tril(jnp.ones((CHUNK, CHUNK), F32)).astype(BF16)
    kernel = functools.partial(_dn_proj_kernel, tm=tm)
    hist = (DN_CONV - 1) * N_HEADS
    return pl.pallas_call(
        kernel,
        grid=(b, t // tm),
        in_specs=[
            pl.BlockSpec((1, tm, d), lambda i, j: (i, j, 0)),
            _const_spec((1, d)),
            _const_spec((3, d, DN_WIDTH)),
            _const_spec((d, LANES)),
            _const_spec((3, DN_CONV, N_HEADS, HEAD_DIM)),
            _const_spec((SUBLANES, LANES)),
            _const_spec((CHUNK, CHUNK)),
        ],
        out_specs=[
            pl.BlockSpec((1, 3, tm, DN_WIDTH), lambda i, j: (i, 0, j, 0)),
            pl.BlockSpec((1, tm, LANES), lambda i, j: (i, j, 0)),
        ],
        out_shape=[
            jax.ShapeDtypeStruct((b, 3, t, DN_WIDTH), BF16),
            jax.ShapeDtypeStruct((b, t, LANES), F32),
        ],
        scratch_shapes=[
            pltpu.VMEM((tm, d), BF16),
            pltpu.VMEM((tm * N_HEADS + hist, HEAD_DIM), F32),
            pltpu.VMEM((tm * N_HEADS, HEAD_DIM), F32),
            pltpu.VMEM((3, hist, HEAD_DIM), F32),
        ],
        compiler_params=pltpu.CompilerParams(
            dimension_semantics=("arbitrary", "arbitrary"), vmem_limit_bytes=VMEM_LIMIT),
        name="dn_proj",
    )(x, norm_g, wqkv, wbd, cw, bdp, ltri)


def _dn_delta_kernel(qkv_ref, bg_ref, o_ref, state_ref, *, tb):
    @pl.when(pl.program_id(1) == 0)
    def _():
        state_ref[...] = jnp.zeros_like(state_ref)

    row = lax.broadcasted_iota(jnp.int32, (CHUNK, CHUNK), 0)
    col = lax.broadcasted_iota(jnp.int32, (CHUNK, CHUNK), 1)
    incl = row >= col
    strict = row > col
    n_levels = CHUNK.bit_length() - 2

    def chunk(ci, carry):
        r0 = pl.multiple_of(ci * CHUNK, CHUNK)
        bg = bg_ref[0, pl.ds(r0, CHUNK), :]
        bg_t = bg.T
        rows = pl.ds(r0, CHUNK)

        def heads(sec):
            return jnp.stack([qkv_ref[0, sec, rows, h * HEAD_DIM:(h + 1) * HEAD_DIM] for h in range(N_HEADS)])

        q, k, v = heads(0), heads(1), heads(2)
        beta = jnp.stack([bg[:, h:h + 1] for h in range(N_HEADS)])
        g_col = jnp.stack([bg[:, N_HEADS + h:N_HEADS + h + 1] for h in range(N_HEADS)])
        g_row = jnp.stack([bg_t[N_HEADS + h:N_HEADS + h + 1, :] for h in range(N_HEADS)])
        g_last = g_col[:, CHUNK - 1:CHUNK, :]
        decay = jnp.where(incl, jnp.exp(jnp.where(incl, g_col - g_row, 0.0)), 0.0)
        kf = k.astype(F32)
        kb = kf * beta
        eg = jnp.exp(g_col)
        kq = _bdot_nt(jnp.concatenate([kb.astype(BF16), q], axis=1), k)
        a = jnp.where(strict, kq[:, :CHUNK] * decay, 0.0)
        qk = jnp.where(incl, kq[:, CHUNK:] * decay, 0.0)
        n = -a
        ab = a.astype(BF16)
        pw = _bdot(ab, ab)
        for level in range(n_levels):
            pwb = pw.astype(BF16)
            if level + 1 < n_levels:
                both = _bdot(jnp.concatenate([n, pw], axis=1).astype(BF16), pwb)
                n = n + pw + both[:, :CHUNK]
                pw = both[:, CHUNK:]
            else:
                n = n + pw + _bdot(n.astype(BF16), pwb)
        rhs = jnp.concatenate([v.astype(F32) * beta, kb * eg], axis=2)
        sol = rhs + _bdot(n.astype(BF16), rhs.astype(BF16))
        u, w = sol[:, :, :HEAD_DIM], sol[:, :, HEAD_DIM:]
        q_g = q.astype(F32) * eg
        k_dt = jnp.swapaxes(kf * jnp.exp(g_last - g_col), 1, 2)
        state = state_ref[...]
        ws = _bdot(jnp.concatenate([w, q_g], axis=1).astype(BF16), state.astype(BF16))
        vb = (u - ws[:, :CHUNK]).astype(BF16)
        ov = _bdot(jnp.concatenate([qk, k_dt], axis=1).astype(BF16), vb)
        state_ref[...] = state * jnp.exp(g_last) + ov[:, CHUNK:]
        o = (ws[:, CHUNK:] + ov[:, :CHUNK]).astype(o_ref.dtype)
        for h in range(N_HEADS):
            o_ref[0, rows, h * LANES:(h + 1) * LANES] = o[h]
        return carry

    lax.fori_loop(0, tb // CHUNK, chunk, 0)


def _dn_delta(qkv, bg, *, tb):
    b, _, t, _ = qkv.shape
    return pl.pallas_call(
        functools.partial(_dn_delta_kernel, tb=tb),
        grid=(b, t // tb),
        in_specs=[
            pl.BlockSpec((1, 3, tb, DN_WIDTH), lambda i, j: (i, 0, j, 0)),
            pl.BlockSpec((1, tb, LANES), lambda i, j: (i, j, 0)),
        ],
        out_specs=pl.BlockSpec((1, tb, DN_WIDTH), lambda i, j: (i, j, 0)),
        out_shape=jax.ShapeDtypeStruct((b, t, DN_WIDTH), BF16),
        scratch_shapes=[pltpu.VMEM((N_HEADS, HEAD_DIM, HEAD_DIM), F32)],
        compiler_params=pltpu.CompilerParams(
            dimension_semantics=("arbitrary", "arbitrary"), vmem_limit_bytes=VMEM_LIMIT),
        name="dn_delta",
    )(qkv, bg)


def _mix_out_kernel(x_ref, o_ref, p_ref, w3_ref, wdn_ref, wcf_ref, wout_ref, wpg_ref, wpp_ref,
                    vec_ref, dww_ref, dwb_ref, out_ref, cbuf_ref, conv_ref, *, tm, rows, final):
    t = pl.program_id(1)
    x = x_ref[0]
    mix_g, dn_g, ln_g, ln_b, ple_g, fin_g = (vec_ref[i:i + 1, :] for i in range(6))
    h = (_rms_scale(x) * mix_g).astype(BF16)

    o = o_ref[0].astype(F32)
    normed = [_rms_scale(o[:, i * HEAD_DIM:(i + 1) * HEAD_DIM]) for i in range(N_HEADS)]
    o_n = jnp.concatenate(normed, axis=1) * dn_g
    a_in = o_n * _silu(_dot(h, w3_ref[0]))
    u_dn = _dot(a_in.astype(BF16), wdn_ref[...])

    c = _dot(h, w3_ref[1]) * _sigmoid(_dot(h, w3_ref[2]))

    hist = CF_HIST * CF_SLABS

    @pl.when(t == 0)
    def _():
        cbuf_ref[0:hist, :] = jnp.zeros((hist, LANES), F32)

    for s in range(CF_SLABS):
        cbuf_ref[pl.ds(hist + s, tm, stride=CF_SLABS), :] = c[:, s * LANES:(s + 1) * LANES]

    first = CF_HIST - (CF_CONV - 1)
    bias = dwb_ref[...]

    for blk in range(tm // rows):
        base = (blk * rows + first) * CF_SLABS
        steps = [cbuf_ref[base + i * CF_SLABS:base + (i + 1) * CF_SLABS, :] for i in range(rows + CF_CONV - 1)]
        acc = [bias] * rows
        for j in range(CF_CONV):
            w = dww_ref[j * CF_SLABS:(j + 1) * CF_SLABS, :]
            acc = [acc[i] + w * steps[i + j] for i in range(rows)]
        out = blk * rows * CF_SLABS
        conv_ref[out:out + rows * CF_SLABS, :] = jnp.concatenate(acc, axis=0)
    cbuf_ref[0:hist, :] = cbuf_ref[tm * CF_SLABS:tm * CF_SLABS + hist, :]

    cv = jnp.concatenate([conv_ref[pl.ds(s, tm, stride=CF_SLABS), :] for s in range(CF_SLABS)], axis=1)
    cc = cv - jnp.mean(cv, axis=-1, keepdims=True)
    ln = cc * lax.rsqrt(jnp.mean(cc * cc, axis=-1, keepdims=True) + EPS) * ln_g + ln_b
    b_in = _silu(ln) * _silu(_dot(h, w3_ref[3]))
    u_cf = _dot(b_in.astype(BF16), wcf_ref[...])

    merged = _sigmoid(_dot(h, w3_ref[4])) * u_dn + _sigmoid(_dot(h, w3_ref[5])) * u_cf
    x1 = x + _dot(merged.astype(BF16), wout_ref[...])

    e = _dot(p_ref[0].astype(BF16), wpp_ref[...])
    gate = _sigmoid(_dot((_rms_scale(x1) * ple_g).astype(BF16), wpg_ref[...]))
    x2 = x1 + gate * e
    if final:
        x2 = _rms_scale(x2) * fin_g
    out_ref[0] = x2


def _mix_out(x, o, p, w3, wdn, wcf, wout, wpg, wpp, vecs, dww, dwb, *, tm, final):
    b, t, d = x.shape
    assert wcf.shape[0] == CF_SLABS * LANES
    rows = CF_TIME_BLOCK
    kernel = functools.partial(_mix_out_kernel, tm=tm, rows=rows, final=final)
    tile = lambda w: pl.BlockSpec((1, tm, w), lambda i, j: (i, j, 0))
    return pl.pallas_call(
        kernel,
        grid=(b, t // tm),
        in_specs=[
            tile(d), tile(o.shape[-1]), tile(p.shape[-1]),
            _const_spec(w3.shape), _const_spec(wdn.shape), _const_spec(wcf.shape),
            _const_spec(wout.shape), _const_spec(wpg.shape), _const_spec(wpp.shape),
            _const_spec(vecs.shape), _const_spec(dww.shape), _const_spec(dwb.shape),
        ],
        out_specs=tile(d),
        out_shape=jax.ShapeDtypeStruct((b, t, d), x.dtype),
        scratch_shapes=[
            pltpu.VMEM(((tm + CF_HIST) * CF_SLABS, LANES), F32),
            pltpu.VMEM((tm * CF_SLABS, LANES), F32),
        ],
        compiler_params=pltpu.CompilerParams(
            dimension_semantics=("arbitrary", "arbitrary"), vmem_limit_bytes=VMEM_LIMIT),
        name="mix_out",
    )(x, o, p, w3, wdn, wcf, wout, wpg, wpp, vecs, dww, dwb)


def _pad_rows(a, n):
    return jnp.pad(a, ((0, n - a.shape[0]), (0, 0)))


def _forward(x, p, mix_norm_g, w_in, dn_conv_w, dn_a_log, dn_dt_bias, dn_out_norm_g, w_dn_out,
             cf_dw_w, cf_dw_b, cf_ln_g, cf_ln_b, w_cf_out, w_out, ple_norm_g, w_ple_gate,
             w_ple_proj, final_norm_g, *, tm1, tb, tm3):
    depth, d, _ = w_in.shape
    cf = w_cf_out.shape[1]
    qkv_w = 3 * DN_WIDTH
    o_z, o_bd = qkv_w, qkv_w + DN_WIDTH
    o_glu = o_bd + 2 * N_HEADS
    o_zcf = o_glu + 2 * cf
    o_gate = o_zcf + cf
    row = lambda v: v.reshape(1, -1).astype(F32)
    for i in range(depth):
        wi = w_in[i]
        wqkv = wi[:, :qkv_w].reshape(d, 3, DN_WIDTH).transpose(1, 0, 2).astype(BF16)
        wbd = jnp.pad(wi[:, o_bd:o_glu], ((0, 0), (0, LANES - 2 * N_HEADS))).astype(BF16)
        cw = dn_conv_w[i].astype(F32).reshape(DN_CONV, 3, N_HEADS, HEAD_DIM).transpose(1, 0, 2, 3)
        pad_heads = lambda v: jnp.pad(v.astype(F32), (N_HEADS, LANES - 2 * N_HEADS)).reshape(1, LANES)
        bdp = _pad_rows(jnp.concatenate([pad_heads(dn_a_log[i]), pad_heads(dn_dt_bias[i])]), SUBLANES)
        w3 = jnp.stack([wi[:, o_z:o_bd], wi[:, o_glu:o_glu + cf], wi[:, o_glu + cf:o_zcf],
                        wi[:, o_zcf:o_gate], wi[:, o_gate:o_gate + d], wi[:, o_gate + d:]]).astype(BF16)
        vecs = _pad_rows(jnp.concatenate([
            row(mix_norm_g[i]), row(jnp.tile(dn_out_norm_g[i], N_HEADS)),
            row(cf_ln_g[i]), row(cf_ln_b[i]), row(ple_norm_g[i]), row(final_norm_g)]), SUBLANES)
        dww = _pad_rows(cf_dw_w[i].astype(F32), CF_HIST).reshape(CF_HIST * CF_SLABS, LANES)
        dwb = cf_dw_b[i].astype(F32).reshape(CF_SLABS, LANES)

        qkv, bg = _dn_proj(x, row(mix_norm_g[i]), wqkv, wbd, cw, bdp, tm=tm1)
        o = _dn_delta(qkv, bg, tb=tb)
        x = _mix_out(x, o, p[i], w3, w_dn_out[i].astype(BF16), w_cf_out[i].astype(BF16),
                     w_out[i].astype(BF16), w_ple_gate[i].astype(BF16), w_ple_proj[i].astype(BF16),
                     vecs, dww, dwb, tm=tm3, final=(i == depth - 1))
    return x


def kernel(x, p, mix_norm_g, w_in, dn_conv_w, dn_a_log, dn_dt_bias, dn_out_norm_g, w_dn_out,
           cf_dw_w, cf_dw_b, cf_ln_g, cf_ln_b, w_cf_out, w_out, ple_norm_g, w_ple_gate,
           w_ple_proj, final_norm_g):
    t = x.shape[1]
    return _forward(x, p, mix_norm_g, w_in, dn_conv_w, dn_a_log, dn_dt_bias, dn_out_norm_g,
                    w_dn_out, cf_dw_w, cf_dw_b, cf_ln_g, cf_ln_b, w_cf_out, w_out, ple_norm_g,
                    w_ple_gate, w_ple_proj, final_norm_g,
                    tm1=min(t, 512), tb=min(t, 512), tm3=min(t, 256))
```

```python
import functools

import jax
import jax.numpy as jnp
from jax import lax
from jax.experimental import pallas as pl
from jax.experimental.pallas import tpu as pltpu

F32 = jnp.float32
BF16 = jnp.bfloat16

EPS = 1e-6
N_HEADS = 8
HEAD_DIM = 128
DN_WIDTH = N_HEADS * HEAD_DIM
DN_CONV = 4
CHUNK = 64
CF_CONV = 31
LANES = 128
SUBLANES = 8
CF_HIST = 32
CF_SLABS = SUBLANES
CF_TIME_BLOCK = 8
DELTA_ROWS = 2
VMEM_LIMIT = 56 * 1024 * 1024


def _sigmoid(x):
    return 1.0 / (1.0 + jnp.exp(-x))


def _silu(x):
    return x * _sigmoid(x)


def _rms_scale(x):
    return x * lax.rsqrt(jnp.mean(x * x, axis=-1, keepdims=True) + EPS)


def _dot(a, b):
    return jnp.dot(a, b, preferred_element_type=F32)


def _bdot(a, b):
    return lax.dot_general(a, b, (((2,), (1,)), ((0,), (0,))), preferred_element_type=F32)


def _bdot_nt(a, b):
    return lax.dot_general(a, b, (((2,), (2,)), ((0,), (0,))), preferred_element_type=F32)


def _const_spec(shape):
    nd = len(shape)
    return pl.BlockSpec(shape, lambda b, t: (0,) * nd, pipeline_mode=pl.Buffered(1))


def _dn_proj_kernel(x_ref, g_ref, wqkv_ref, wbd_ref, cw_ref, bdp_ref, ltri_ref,
                    qkv_ref, bg_ref, h_ref, pz_ref, yz_ref, tail_ref, *, tm):
    t = pl.program_id(1)
    h_ref[...] = (_rms_scale(x_ref[0]) * g_ref[...]).astype(BF16)

    @pl.when(t == 0)
    def _():
        tail_ref[...] = jnp.zeros_like(tail_ref)

    hist = (DN_CONV - 1) * N_HEADS
    for sec in range(3):
        pre = _dot(h_ref[...], wqkv_ref[sec])
        pz_ref[0:hist, :] = tail_ref[sec]
        for h in range(N_HEADS):
            pz_ref[pl.ds(hist + h, tm, stride=N_HEADS), :] = pre[:, h * HEAD_DIM:(h + 1) * HEAD_DIM]
        tail_ref[sec] = pz_ref[tm * N_HEADS:tm * N_HEADS + hist, :]
        p3 = pz_ref[...].reshape(tm + DN_CONV - 1, N_HEADS, HEAD_DIM)
        acc = None
        for j in range(DN_CONV):
            term = p3[j:j + tm] * cw_ref[sec, j]
            acc = term if acc is None else acc + term
        y = _silu(acc)
        if sec < 2:
            inv = lax.rsqrt(jnp.sum(y * y, axis=-1, keepdims=True) + EPS)
            y = y * (inv * HEAD_DIM ** -0.5 if sec == 0 else inv)
        yz_ref[...] = y.reshape(tm * N_HEADS, HEAD_DIM)
        qkv_ref[0, sec] = jnp.concatenate(
            [yz_ref[pl.ds(h, tm, stride=N_HEADS), :] for h in range(N_HEADS)], axis=1).astype(BF16)

    logits = _dot(h_ref[...], wbd_ref[...])
    lane = lax.broadcasted_iota(jnp.int32, (CHUNK, LANES), 1)
    z = logits + bdp_ref[1:2, :]
    softplus = jnp.maximum(z, 0.0) + jnp.log1p(jnp.exp(-jnp.abs(z)))
    log_decay = -jnp.exp(bdp_ref[0:1, :]) * softplus
    beta = _sigmoid(logits)
    ltri = ltri_ref[...]
    for ch in range(tm // CHUNK):
        sl = slice(ch * CHUNK, (ch + 1) * CHUNK)
        seg = jnp.where(lane >= N_HEADS, log_decay[sl, :], 0.0)
        hi = seg.astype(BF16)
        r1 = seg - hi.astype(F32)
        mid = r1.astype(BF16)
        lo = (r1 - mid.astype(F32)).astype(BF16)
        g = _dot(ltri, hi) + _dot(ltri, mid) + _dot(ltri, lo)
        bg_ref[0, sl, :] = jnp.where(lane < N_HEADS, beta[sl, :], g)


def _dn_proj(x, norm_g, wqkv, wbd, cw, bdp, *, tm):
    b, t, d = x.shape
    ltri = jnp.tril(jnp.ones((CHUNK, CHUNK), F32)).astype(BF16)
    kernel = functools.partial(_dn_proj_kernel, tm=tm)
    hist = (DN_CONV - 1) * N_HEADS
    return pl.pallas_call(
        kernel,
        grid=(b, t // tm),
        in_specs=[
            pl.BlockSpec((1, tm, d), lambda i, j: (i, j, 0)),
            _const_spec((1, d)),
            _const_spec((3, d, DN_WIDTH)),
            _const_spec((d, LANES)),
            _const_spec((3, DN_CONV, N_HEADS, HEAD_DIM)),
            _const_spec((SUBLANES, LANES)),
            _const_spec((CHUNK, CHUNK)),
        ],
        out_specs=[
            pl.BlockSpec((1, 3, tm, DN_WIDTH), lambda i, j: (i, 0, j, 0)),
            pl.BlockSpec((1, tm, LANES), lambda i, j: (i, j, 0)),
        ],
        out_shape=[
            jax.ShapeDtypeStruct((b, 3, t, DN_WIDTH), BF16),
            jax.ShapeDtypeStruct((b, t, LANES), F32),
        ],
        scratch_shapes=[
            pltpu.VMEM((tm, d), BF16),
            pltpu.VMEM((tm * N_HEADS + hist, HEAD_DIM), F32),
            pltpu.VMEM((tm * N_HEADS, HEAD_DIM), F32),
            pltpu.VMEM((3, hist, HEAD_DIM), F32),
        ],
        compiler_params=pltpu.CompilerParams(
            dimension_semantics=("arbitrary", "arbitrary"), vmem_limit_bytes=VMEM_LIMIT),
        name="dn_proj",
    )(x, norm_g, wqkv, wbd, cw, bdp, ltri)


def _dn_delta_kernel(qkv_ref, bg_ref, o_ref, state_ref, *, tb, nb):
    @pl.when(pl.program_id(1) == 0)
    def _():
        state_ref[...] = jnp.zeros_like(state_ref)

    row = lax.broadcasted_iota(jnp.int32, (CHUNK, CHUNK), 0)
    col = lax.broadcasted_iota(jnp.int32, (CHUNK, CHUNK), 1)
    incl = row >= col
    strict = row > col
    n_levels = CHUNK.bit_length() - 2

    def chunk(ci, carry):
        r0 = pl.multiple_of(ci * CHUNK, CHUNK)
        rows = pl.ds(r0, CHUNK)
        bg = [bg_ref[i, rows, :] for i in range(nb)]
        bg_t = [x.T for x in bg]
        pairs = [(i, h) for i in range(nb) for h in range(N_HEADS)]

        def heads(sec):
            return jnp.stack([qkv_ref[i, sec, rows, h * HEAD_DIM:(h + 1) * HEAD_DIM] for i, h in pairs])

        q, k, v = heads(0), heads(1), heads(2)
        beta = jnp.stack([bg[i][:, h:h + 1] for i, h in pairs])
        g_col = jnp.stack([bg[i][:, N_HEADS + h:N_HEADS + h + 1] for i, h in pairs])
        g_row = jnp.stack([bg_t[i][N_HEADS + h:N_HEADS + h + 1, :] for i, h in pairs])
        g_last = g_col[:, CHUNK - 1:CHUNK, :]
        decay = jnp.where(incl, jnp.exp(jnp.where(incl, g_col - g_row, 0.0)), 0.0)
        kf = k.astype(F32)
        kb = kf * beta
        eg = jnp.exp(g_col)
        kq = _bdot_nt(jnp.concatenate([kb.astype(BF16), q], axis=1), k)
        a = jnp.where(strict, kq[:, :CHUNK] * decay, 0.0)
        qk = jnp.where(incl, kq[:, CHUNK:] * decay, 0.0)
        n = -a
        ab = a.astype(BF16)
        pw = _bdot(ab, ab)
        for level in range(n_levels):
            pwb = pw.astype(BF16)
            if level + 1 < n_levels:
                both = _bdot(jnp.concatenate([n, pw], axis=1).astype(BF16), pwb)
                n = n + pw + both[:, :CHUNK]
                pw = both[:, CHUNK:]
            else:
                n = n + pw + _bdot(n.astype(BF16), pwb)
        rhs = jnp.concatenate([v.astype(F32) * beta, kb * eg], axis=2)
        sol = rhs + _bdot(n.astype(BF16), rhs.astype(BF16))
        u, w = sol[:, :, :HEAD_DIM], sol[:, :, HEAD_DIM:]
        q_g = q.astype(F32) * eg
        k_dt = jnp.swapaxes(kf * jnp.exp(g_last - g_col), 1, 2)
        state = state_ref[...]
        ws = _bdot(jnp.concatenate([w, q_g], axis=1).astype(BF16), state.astype(BF16))
        vb = (u - ws[:, :CHUNK]).astype(BF16)
        ov = _bdot(jnp.concatenate([qk, k_dt], axis=1).astype(BF16), vb)
        state_ref[...] = state * jnp.exp(g_last) + ov[:, CHUNK:]
        o = (ws[:, CHUNK:] + ov[:, :CHUNK]).astype(o_ref.dtype)
        for n_pair, (i, h) in enumerate(pairs):
            o_ref[i, rows, h * HEAD_DIM:(h + 1) * HEAD_DIM] = o[n_pair]
        return carry

    lax.fori_loop(0, tb // CHUNK, chunk, 0)


def _dn_delta(qkv, bg, *, tb, nb):
    b, _, t, _ = qkv.shape
    return pl.pallas_call(
        functools.partial(_dn_delta_kernel, tb=tb, nb=nb),
        grid=(b // nb, t // tb),
        in_specs=[
            pl.BlockSpec((nb, 3, tb, DN_WIDTH), lambda i, j: (i, 0, j, 0)),
            pl.BlockSpec((nb, tb, LANES), lambda i, j: (i, j, 0)),
        ],
        out_specs=pl.BlockSpec((nb, tb, DN_WIDTH), lambda i, j: (i, j, 0)),
        out_shape=jax.ShapeDtypeStruct((b, t, DN_WIDTH), BF16),
        scratch_shapes=[pltpu.VMEM((nb * N_HEADS, HEAD_DIM, HEAD_DIM), F32)],
        compiler_params=pltpu.CompilerParams(
            dimension_semantics=("arbitrary", "arbitrary"), vmem_limit_bytes=VMEM_LIMIT),
        name="dn_delta",
    )(qkv, bg)


def _mix_out_kernel(x_ref, o_ref, p_ref, w3_ref, wdn_ref, wcf_ref, wout_ref, wpg_ref, wpp_ref,
                    vec_ref, dww_ref, dwb_ref, out_ref, cbuf_ref, conv_ref, *, tm, rows, final):
    hist = CF_HIST * CF_SLABS

    @pl.when(pl.program_id(1) == 0)
    def _():
        cbuf_ref[0:hist, :] = jnp.zeros((hist, LANES), F32)

    x = x_ref[0]
    mix_g, dn_g, ln_g, ln_b, ple_g, fin_g = (vec_ref[i:i + 1, :] for i in range(6))
    h = (_rms_scale(x) * mix_g).astype(BF16)

    c = _dot(h, w3_ref[1]) * _sigmoid(_dot(h, w3_ref[2]))
    for s in range(CF_SLABS):
        cbuf_ref[pl.ds(hist + s, tm, stride=CF_SLABS), :] = c[:, s * LANES:(s + 1) * LANES]

    first = CF_HIST - (CF_CONV - 1)
    bias = dwb_ref[...]
    for blk in range(tm // rows):
        base = (blk * rows + first) * CF_SLABS
        steps = [cbuf_ref[base + i * CF_SLABS:base + (i + 1) * CF_SLABS, :] for i in range(rows + CF_CONV - 1)]
        acc = [bias] * rows
        for j in range(CF_CONV):
            w = dww_ref[j * CF_SLABS:(j + 1) * CF_SLABS, :]
            acc = [acc[i] + w * steps[i + j] for i in range(rows)]
        out = blk * rows * CF_SLABS
        conv_ref[out:out + rows * CF_SLABS, :] = jnp.concatenate(acc, axis=0)
    cbuf_ref[0:hist, :] = cbuf_ref[tm * CF_SLABS:tm * CF_SLABS + hist, :]

    o = o_ref[0].astype(F32)
    normed = [_rms_scale(o[:, i * HEAD_DIM:(i + 1) * HEAD_DIM]) for i in range(N_HEADS)]
    o_n = jnp.concatenate(normed, axis=1) * dn_g
    a_in = o_n * _silu(_dot(h, w3_ref[0]))
    u_dn = _dot(a_in.astype(BF16), wdn_ref[...])

    cv = jnp.concatenate([conv_ref[pl.ds(s, tm, stride=CF_SLABS), :] for s in range(CF_SLABS)], axis=1)
    cc = cv - jnp.mean(cv, axis=-1, keepdims=True)
    ln = cc * lax.rsqrt(jnp.mean(cc * cc, axis=-1, keepdims=True) + EPS) * ln_g + ln_b
    b_in = _silu(ln) * _silu(_dot(h, w3_ref[3]))
    u_cf = _dot(b_in.astype(BF16), wcf_ref[...])

    merged = _sigmoid(_dot(h, w3_ref[4])) * u_dn + _sigmoid(_dot(h, w3_ref[5])) * u_cf
    x1 = x + _dot(merged.astype(BF16), wout_ref[...])

    e = _dot(p_ref[0].astype(BF16), wpp_ref[...])
    gate = _sigmoid(_dot((_rms_scale(x1) * ple_g).astype(BF16), wpg_ref[...]))
    x2 = x1 + gate * e
    if final:
        x2 = _rms_scale(x2) * fin_g
    out_ref[0] = x2


def _mix_out(x, o, p, w3, wdn, wcf, wout, wpg, wpp, vecs, dww, dwb, *, tm, final):
    b, t, d = x.shape
    assert wcf.shape[0] == CF_SLABS * LANES
    rows = CF_TIME_BLOCK
    kernel = functools.partial(_mix_out_kernel, tm=tm, rows=rows, final=final)
    tile = lambda w: pl.BlockSpec((1, tm, w), lambda i, j: (i, j, 0))
    return pl.pallas_call(
        kernel,
        grid=(b, t // tm),
        in_specs=[
            tile(d), tile(o.shape[-1]), tile(p.shape[-1]),
            _const_spec(w3.shape), _const_spec(wdn.shape), _const_spec(wcf.shape),
            _const_spec(wout.shape), _const_spec(wpg.shape), _const_spec(wpp.shape),
            _const_spec(vecs.shape), _const_spec(dww.shape), _const_spec(dwb.shape),
        ],
        out_specs=tile(d),
        out_shape=jax.ShapeDtypeStruct((b, t, d), x.dtype),
        scratch_shapes=[
            pltpu.VMEM(((tm + CF_HIST) * CF_SLABS, LANES), F32),
            pltpu.VMEM((tm * CF_SLABS, LANES), F32),
        ],
        compiler_params=pltpu.CompilerParams(
            dimension_semantics=("arbitrary", "arbitrary"), vmem_limit_bytes=VMEM_LIMIT),
        name="mix_out",
    )(x, o, p, w3, wdn, wcf, wout, wpg, wpp, vecs, dww, dwb)


def _pad_rows(a, n):
    return jnp.pad(a, ((0, n - a.shape[0]), (0, 0)))


def _forward(x, p, mix_norm_g, w_in, dn_conv_w, dn_a_log, dn_dt_bias, dn_out_norm_g, w_dn_out,
             cf_dw_w, cf_dw_b, cf_ln_g, cf_ln_b, w_cf_out, w_out, ple_norm_g, w_ple_gate,
             w_ple_proj, final_norm_g, *, tm1, tb, tm3):
    depth, d, _ = w_in.shape
    cf = w_cf_out.shape[1]
    qkv_w = 3 * DN_WIDTH
    o_z, o_bd = qkv_w, qkv_w + DN_WIDTH
    o_glu = o_bd + 2 * N_HEADS
    o_zcf = o_glu + 2 * cf
    o_gate = o_zcf + cf
    row = lambda v: v.reshape(1, -1).astype(F32)
    for i in range(depth):
        wi = w_in[i]
        wqkv = wi[:, :qkv_w].reshape(d, 3, DN_WIDTH).transpose(1, 0, 2).astype(BF16)
        wbd = jnp.pad(wi[:, o_bd:o_glu], ((0, 0), (0, LANES - 2 * N_HEADS))).astype(BF16)
        cw = dn_conv_w[i].astype(F32).reshape(DN_CONV, 3, N_HEADS, HEAD_DIM).transpose(1, 0, 2, 3)
        pad_heads = lambda v: jnp.pad(v.astype(F32), (N_HEADS, LANES - 2 * N_HEADS)).reshape(1, LANES)
        bdp = _pad_rows(jnp.concatenate([pad_heads(dn_a_log[i]), pad_heads(dn_dt_bias[i])]), SUBLANES)
        w3 = jnp.stack([wi[:, o_z:o_bd], wi[:, o_glu:o_glu + cf], wi[:, o_glu + cf:o_zcf],
                        wi[:, o_zcf:o_gate], wi[:, o_gate:o_gate + d], wi[:, o_gate + d:]]).astype(BF16)
        vecs = _pad_rows(jnp.concatenate([
            row(mix_norm_g[i]), row(jnp.tile(dn_out_norm_g[i], N_HEADS)),
            row(cf_ln_g[i]), row(cf_ln_b[i]), row(ple_norm_g[i]), row(final_norm_g)]), SUBLANES)
        dww = _pad_rows(cf_dw_w[i].astype(F32), CF_HIST).reshape(CF_HIST * CF_SLABS, LANES)
        dwb = cf_dw_b[i].astype(F32).reshape(CF_SLABS, LANES)

        qkv, bg = _dn_proj(x, row(mix_norm_g[i]), wqkv, wbd, cw, bdp, tm=tm1)
        o = _dn_delta(qkv, bg, tb=tb, nb=DELTA_ROWS if x.shape[0] % DELTA_ROWS == 0 else 1)
        x = _mix_out(x, o, p[i], w3, w_dn_out[i].astype(BF16), w_cf_out[i].astype(BF16),
                     w_out[i].astype(BF16), w_ple_gate[i].astype(BF16), w_ple_proj[i].astype(BF16),
                     vecs, dww, dwb, tm=tm3, final=(i == depth - 1))
    return x


def kernel(x, p, mix_norm_g, w_in, dn_conv_w, dn_a_log, dn_dt_bias, dn_out_norm_g, w_dn_out,
           cf_dw_w, cf_dw_b, cf_ln_g, cf_ln_b, w_cf_out, w_out, ple_norm_g, w_ple_gate,
           w_ple_proj, final_norm_g):
    t = x.shape[1]
    return _forward(x, p, mix_norm_g, w_in, dn_conv_w, dn_a_log, dn_dt_bias, dn_out_norm_g,
                    w_dn_out, cf_dw_w, cf_dw_b, cf_ln_g, cf_ln_b, w_cf_out, w_out, ple_norm_g,
                    w_ple_gate, w_ple_proj, final_norm_g,
                    tm1=min(t, 512), tb=min(t, 512), tm3=min(t, 256))
```

```python
import functools

import jax
import jax.numpy as jnp
from jax import lax
from jax.experimental import pallas as pl
from jax.experimental.pallas import tpu as pltpu

F32 = jnp.float32
BF16 = jnp.bfloat16

EPS = 1e-6
N_HEADS = 8
HEAD_DIM = 128
DN_WIDTH = N_HEADS * HEAD_DIM
DN_CONV = 4
CHUNK = 64
CF_CONV = 31
LANES = 128
SUBLANES = 8
CF_HIST = 32
CF_SLABS = SUBLANES
CF_TIME_BLOCK = 8
MIX_SUB_TILE = 256
TIME_TILE = 512
DELTA_ROWS = 2
VMEM_LIMIT = 56 * 1024 * 1024


def _sigmoid(x):
    return 0.5 * jnp.tanh(0.5 * x) + 0.5


def _silu(x):
    half = 0.5 * x
    return half * jnp.tanh(half) + half


def _rms_scale(x):
    return x * lax.rsqrt(jnp.mean(x * x, axis=-1, keepdims=True) + EPS)


def _dot(a, b):
    return jnp.dot(a, b, preferred_element_type=F32)


def _bdot(a, b):
    return lax.dot_general(a, b, (((2,), (1,)), ((0,), (0,))), preferred_element_type=F32)


def _bdot_nt(a, b):
    return lax.dot_general(a, b, (((2,), (2,)), ((0,), (0,))), preferred_element_type=F32)


def _const_spec(shape):
    nd = len(shape)
    return pl.BlockSpec(shape, lambda *_: (0,) * nd, pipeline_mode=pl.Buffered(1))


def _dn_proj_kernel(x_ref, g_ref, wqkv_ref, wbd_ref, cw_ref, bdp_ref, ltri_ref,
                    qkv_ref, bg_ref, h_ref, pz_ref, yz_ref, tail_ref, *, tm):
    t = pl.program_id(1)
    h_ref[...] = (_rms_scale(x_ref[0]) * g_ref[...]).astype(BF16)

    @pl.when(t == 0)
    def _():
        tail_ref[...] = jnp.zeros_like(tail_ref)

    hist = (DN_CONV - 1) * N_HEADS
    for sec in range(3):
        pre = _dot(h_ref[...], wqkv_ref[:, sec * DN_WIDTH:(sec + 1) * DN_WIDTH])
        pz_ref[0:hist, :] = tail_ref[sec]
        for h in range(N_HEADS):
            pz_ref[pl.ds(hist + h, tm, stride=N_HEADS), :] = pre[:, h * HEAD_DIM:(h + 1) * HEAD_DIM]
        tail_ref[sec] = pz_ref[tm * N_HEADS:tm * N_HEADS + hist, :]
        p3 = pz_ref[...].reshape(tm + DN_CONV - 1, N_HEADS, HEAD_DIM)
        acc = None
        for j in range(DN_CONV):
            term = p3[j:j + tm] * cw_ref[sec, j]
            acc = term if acc is None else acc + term
        y = _silu(acc)
        if sec < 2:
            inv = lax.rsqrt(jnp.sum(y * y, axis=-1, keepdims=True) + EPS)
            y = y * (inv * HEAD_DIM ** -0.5 if sec == 0 else inv)
        yz_ref[...] = y.reshape(tm * N_HEADS, HEAD_DIM)
        qkv_ref[0, sec] = jnp.concatenate(
            [yz_ref[pl.ds(h, tm, stride=N_HEADS), :] for h in range(N_HEADS)], axis=1).astype(BF16)

    logits = _dot(h_ref[...], wbd_ref[...])
    lane = lax.broadcasted_iota(jnp.int32, (CHUNK, LANES), 1)
    z = logits + bdp_ref[1:2, :]
    softplus = jnp.maximum(z, 0.0) + jnp.log1p(jnp.exp(-jnp.abs(z)))
    log_decay = -jnp.exp(bdp_ref[0:1, :]) * softplus
    beta = _sigmoid(logits)
    ltri = ltri_ref[...]
    for ch in range(tm // CHUNK):
        sl = slice(ch * CHUNK, (ch + 1) * CHUNK)
        seg = jnp.where(lane >= N_HEADS, log_decay[sl, :], 0.0)
        hi = seg.astype(BF16)
        r1 = seg - hi.astype(F32)
        mid = r1.astype(BF16)
        lo = (r1 - mid.astype(F32)).astype(BF16)
        g = _dot(ltri, hi) + _dot(ltri, mid) + _dot(ltri, lo)
        bg_ref[0, sl, :] = jnp.where(lane < N_HEADS, beta[sl, :], g)


def _dn_proj(x, norm_g, wqkv, wbd, cw, bdp, *, tm):
    b, t, d = x.shape
    ltri = jnp.tril(jnp.ones((CHUNK, CHUNK), F32)).astype(BF16)
    kernel = functools.partial(_dn_proj_kernel, tm=tm)
    hist = (DN_CONV - 1) * N_HEADS
    return pl.pallas_call(
        kernel,
        grid=(b, t // tm),
        in_specs=[
            pl.BlockSpec((1, tm, d), lambda i, j: (i, j, 0)),
            _const_spec((1, d)),
            _const_spec((d, 3 * DN_WIDTH)),
            _const_spec((d, LANES)),
            _const_spec((3, DN_CONV, N_HEADS, HEAD_DIM)),
            _const_spec((SUBLANES, LANES)),
            _const_spec((CHUNK, CHUNK)),
        ],
        out_specs=[
            pl.BlockSpec((1, 3, tm, DN_WIDTH), lambda i, j: (i, 0, j, 0)),
            pl.BlockSpec((1, tm, LANES), lambda i, j: (i, j, 0)),
        ],
        out_shape=[
            jax.ShapeDtypeStruct((b, 3, t, DN_WIDTH), BF16),
            jax.ShapeDtypeStruct((b, t, LANES), F32),
        ],
        scratch_shapes=[
            pltpu.VMEM((tm, d), BF16),
            pltpu.VMEM((tm * N_HEADS + hist, HEAD_DIM), F32),
            pltpu.VMEM((tm * N_HEADS, HEAD_DIM), F32),
            pltpu.VMEM((3, hist, HEAD_DIM), F32),
        ],
        compiler_params=pltpu.CompilerParams(
            dimension_semantics=("arbitrary", "arbitrary"), vmem_limit_bytes=VMEM_LIMIT),
        name="dn_proj",
    )(x, norm_g, wqkv, wbd, cw, bdp, ltri)


def _dn_delta_kernel(qkv_ref, bg_ref, o_ref, state_ref, *, tb, nb):
    @pl.when(pl.program_id(1) == 0)
    def _():
        state_ref[...] = jnp.zeros_like(state_ref)

    row = lax.broadcasted_iota(jnp.int32, (CHUNK, CHUNK), 0)
    col = lax.broadcasted_iota(jnp.int32, (CHUNK, CHUNK), 1)
    incl = row >= col
    strict = row > col
    n_levels = CHUNK.bit_length() - 2

    def chunk(ci, carry):
        r0 = pl.multiple_of(ci * CHUNK, CHUNK)
        rows = pl.ds(r0, CHUNK)
        bg = [bg_ref[i, rows, :] for i in range(nb)]
        bg_t = [x.T for x in bg]
        pairs = [(i, h) for i in range(nb) for h in range(N_HEADS)]

        def heads(sec):
            return jnp.stack([qkv_ref[i, sec, rows, h * HEAD_DIM:(h + 1) * HEAD_DIM] for i, h in pairs])

        q, k, v = heads(0), heads(1), heads(2)
        beta = jnp.stack([bg[i][:, h:h + 1] for i, h in pairs])
        g_col = jnp.stack([bg[i][:, N_HEADS + h:N_HEADS + h + 1] for i, h in pairs])
        g_row = jnp.stack([bg_t[i][N_HEADS + h:N_HEADS + h + 1, :] for i, h in pairs])
        g_last = g_col[:, CHUNK - 1:CHUNK, :]
        decay = jnp.where(incl, jnp.exp(jnp.where(incl, g_col - g_row, 0.0)), 0.0)
        kf = k.astype(F32)
        kb = kf * beta
        eg = jnp.exp(g_col)
        kq = _bdot_nt(jnp.concatenate([kb.astype(BF16), q], axis=1), k)
        a = jnp.where(strict, kq[:, :CHUNK] * decay, 0.0)
        qk = jnp.where(incl, kq[:, CHUNK:] * decay, 0.0)
        n = -a
        ab = a.astype(BF16)
        pw = _bdot(ab, ab)
        for level in range(n_levels):
            pwb = pw.astype(BF16)
            if level + 1 < n_levels:
                both = _bdot(jnp.concatenate([n, pw], axis=1).astype(BF16), pwb)
                n = n + pw + both[:, :CHUNK]
                pw = both[:, CHUNK:]
            else:
                n = n + pw + _bdot(n.astype(BF16), pwb)
        rhs = jnp.concatenate([v.astype(F32) * beta, kb * eg], axis=2)
        sol = rhs + _bdot(n.astype(BF16), rhs.astype(BF16))
        u, w = sol[:, :, :HEAD_DIM], sol[:, :, HEAD_DIM:]
        q_g = q.astype(F32) * eg
        k_dt = jnp.swapaxes(kf * jnp.exp(g_last - g_col), 1, 2)
        state = state_ref[...]
        ws = _bdot(jnp.concatenate([w, q_g], axis=1).astype(BF16), state.astype(BF16))
        vb = (u - ws[:, :CHUNK]).astype(BF16)
        ov = _bdot(jnp.concatenate([qk, k_dt], axis=1).astype(BF16), vb)
        state_ref[...] = state * jnp.exp(g_last) + ov[:, CHUNK:]
        o = (ws[:, CHUNK:] + ov[:, :CHUNK]).astype(o_ref.dtype)
        for n_pair, (i, h) in enumerate(pairs):
            o_ref[i, rows, h * HEAD_DIM:(h + 1) * HEAD_DIM] = o[n_pair]
        return carry

    lax.fori_loop(0, tb // CHUNK, chunk, 0)


def _dn_delta(qkv, bg, *, tb, nb):
    b, _, t, _ = qkv.shape
    return pl.pallas_call(
        functools.partial(_dn_delta_kernel, tb=tb, nb=nb),
        grid=(b // nb, t // tb),
        in_specs=[
            pl.BlockSpec((nb, 3, tb, DN_WIDTH), lambda i, j: (i, 0, j, 0)),
            pl.BlockSpec((nb, tb, LANES), lambda i, j: (i, j, 0)),
        ],
        out_specs=pl.BlockSpec((nb, tb, DN_WIDTH), lambda i, j: (i, j, 0)),
        out_shape=jax.ShapeDtypeStruct((b, t, DN_WIDTH), BF16),
        scratch_shapes=[pltpu.VMEM((nb * N_HEADS, HEAD_DIM, HEAD_DIM), F32)],
        compiler_params=pltpu.CompilerParams(
            dimension_semantics=("arbitrary", "arbitrary"), vmem_limit_bytes=VMEM_LIMIT),
        name="dn_delta",
    )(qkv, bg)


def _mix_out_kernel(x_ref, o_ref, p_ref, w3_ref, wdn_ref, wcf_ref, wout_ref, wpg_ref, wpp_ref,
                    vec_ref, dww_ref, dwb_ref, out_ref, cbuf_ref, conv_ref, *, tm, sub, rows, final):
    hist = CF_HIST * CF_SLABS

    @pl.when(pl.program_id(1) == 0)
    def _():
        cbuf_ref[0:hist, :] = jnp.zeros((hist, LANES), F32)

    mix_g, dn_g, ln_g, ln_b, ple_g, fin_g = (vec_ref[i:i + 1, :] for i in range(6))
    first = CF_HIST - (CF_CONV - 1)
    bias = dwb_ref[...]

    def conv_blocks(lo, hi):
        for blk in range(lo, hi):
            base = (blk * rows + first) * CF_SLABS
            steps = [cbuf_ref[base + i * CF_SLABS:base + (i + 1) * CF_SLABS, :]
                     for i in range(rows + CF_CONV - 1)]
            acc = [bias] * rows
            for j in range(CF_CONV):
                w = dww_ref[j * CF_SLABS:(j + 1) * CF_SLABS, :]
                acc = [acc[i] + w * steps[i + j] for i in range(rows)]
            out = blk * rows * CF_SLABS
            conv_ref[out:out + rows * CF_SLABS, :] = jnp.concatenate(acc, axis=0)

    def head(r0):
        x = x_ref[0, r0:r0 + sub, :]
        h = (_rms_scale(x) * mix_g).astype(BF16)
        c = _dot(h, w3_ref[1]) * _sigmoid(_dot(h, w3_ref[2]))
        for s in range(CF_SLABS):
            cbuf_ref[pl.ds(hist + r0 * CF_SLABS + s, sub, stride=CF_SLABS), :] = c[:, s * LANES:(s + 1) * LANES]
        return x, h

    def tail(r0, x, h):
        rs = slice(r0, r0 + sub)
        o = o_ref[0, rs, :].astype(F32)
        normed = [_rms_scale(o[:, i * HEAD_DIM:(i + 1) * HEAD_DIM]) for i in range(N_HEADS)]
        o_n = jnp.concatenate(normed, axis=1) * dn_g
        a_in = o_n * _silu(_dot(h, w3_ref[0]))
        u_dn = _dot(a_in.astype(BF16), wdn_ref[...])
        yield
        cv = jnp.concatenate(
            [conv_ref[pl.ds(r0 * CF_SLABS + s, sub, stride=CF_SLABS), :] for s in range(CF_SLABS)], axis=1)
        cc = cv - jnp.mean(cv, axis=-1, keepdims=True)
        ln = cc * lax.rsqrt(jnp.mean(cc * cc, axis=-1, keepdims=True) + EPS) * ln_g + ln_b
        b_in = _silu(ln) * _silu(_dot(h, w3_ref[3]))
        u_cf = _dot(b_in.astype(BF16), wcf_ref[...])
        yield
        merged = _sigmoid(_dot(h, w3_ref[4])) * u_dn + _sigmoid(_dot(h, w3_ref[5])) * u_cf
        x1 = x + _dot(merged.astype(BF16), wout_ref[...])
        yield
        e = _dot(p_ref[0, rs, :].astype(BF16), wpp_ref[...])
        gate = _sigmoid(_dot((_rms_scale(x1) * ple_g).astype(BF16), wpg_ref[...]))
        x2 = x1 + gate * e
        if final:
            x2 = _rms_scale(x2) * fin_g
        out_ref[0, rs, :] = x2
        yield

    n_quarters = 4
    blocks = sub // rows
    pending = None
    for r0 in range(0, tm, sub):
        x, h = head(r0)
        b0 = r0 // rows
        for q in range(n_quarters):
            if pending is not None:
                next(pending)
            conv_blocks(b0 + blocks * q // n_quarters, b0 + blocks * (q + 1) // n_quarters)
        pending = tail(r0, x, h)
    for _ in range(n_quarters):
        next(pending)
    cbuf_ref[0:hist, :] = cbuf_ref[tm * CF_SLABS:tm * CF_SLABS + hist, :]


def _mix_out(x, o, p, w3, wdn, wcf, wout, wpg, wpp, vecs, dww, dwb, *, tm, final):
    b, t, d = x.shape
    assert wcf.shape[0] == CF_SLABS * LANES
    rows = CF_TIME_BLOCK
    kernel = functools.partial(_mix_out_kernel, tm=tm, sub=min(tm, MIX_SUB_TILE), rows=rows, final=final)
    tile = lambda w: pl.BlockSpec((1, tm, w), lambda i, j: (i, j, 0))
    return pl.pallas_call(
        kernel,
        grid=(b, t // tm),
        in_specs=[
            tile(d), tile(o.shape[-1]), tile(p.shape[-1]),
            _const_spec(w3.shape), _const_spec(wdn.shape), _const_spec(wcf.shape),
            _const_spec(wout.shape), _const_spec(wpg.shape), _const_spec(wpp.shape),
            _const_spec(vecs.shape), _const_spec(dww.shape), _const_spec(dwb.shape),
        ],
        out_specs=tile(d),
        out_shape=jax.ShapeDtypeStruct((b, t, d), x.dtype),
        scratch_shapes=[
            pltpu.VMEM(((tm + CF_HIST) * CF_SLABS, LANES), F32),
            pltpu.VMEM((tm * CF_SLABS, LANES), F32),
        ],
        compiler_params=pltpu.CompilerParams(
            dimension_semantics=("arbitrary", "arbitrary"), vmem_limit_bytes=VMEM_LIMIT),
        name="mix_out",
    )(x, o, p, w3, wdn, wcf, wout, wpg, wpp, vecs, dww, dwb)


def _pad_rows(a, n):
    return jnp.pad(a, ((0, n - a.shape[0]), (0, 0)))


def _forward(x, p, mix_norm_g, w_in, dn_conv_w, dn_a_log, dn_dt_bias, dn_out_norm_g, w_dn_out,
             cf_dw_w, cf_dw_b, cf_ln_g, cf_ln_b, w_cf_out, w_out, ple_norm_g, w_ple_gate,
             w_ple_proj, final_norm_g, *, tm1, tb, tm3):
    depth, d, _ = w_in.shape
    cf = w_cf_out.shape[1]
    qkv_w = 3 * DN_WIDTH
    o_z, o_bd = qkv_w, qkv_w + DN_WIDTH
    o_glu = o_bd + 2 * N_HEADS
    o_zcf = o_glu + 2 * cf
    o_gate = o_zcf + cf
    row = lambda v: v.reshape(1, -1).astype(F32)
    for i in range(depth):
        wi = w_in[i]
        wqkv = wi[:, :qkv_w].astype(BF16)
        wbd = jnp.pad(wi[:, o_bd:o_glu], ((0, 0), (0, LANES - 2 * N_HEADS))).astype(BF16)
        cw = dn_conv_w[i].astype(F32).reshape(DN_CONV, 3, N_HEADS, HEAD_DIM).transpose(1, 0, 2, 3)
        pad_heads = lambda v: jnp.pad(v.astype(F32), (N_HEADS, LANES - 2 * N_HEADS)).reshape(1, LANES)
        bdp = _pad_rows(jnp.concatenate([pad_heads(dn_a_log[i]), pad_heads(dn_dt_bias[i])]), SUBLANES)
        w3 = jnp.stack([wi[:, o_z:o_bd], wi[:, o_glu:o_glu + cf], wi[:, o_glu + cf:o_zcf],
                        wi[:, o_zcf:o_gate], wi[:, o_gate:o_gate + d], wi[:, o_gate + d:]]).astype(BF16)
        vecs = _pad_rows(jnp.concatenate([
            row(mix_norm_g[i]), row(jnp.tile(dn_out_norm_g[i], N_HEADS)),
            row(cf_ln_g[i]), row(cf_ln_b[i]), row(ple_norm_g[i]), row(final_norm_g)]), SUBLANES)
        dww = _pad_rows(cf_dw_w[i].astype(F32), CF_HIST).reshape(CF_HIST * CF_SLABS, LANES)
        dwb = cf_dw_b[i].astype(F32).reshape(CF_SLABS, LANES)

        qkv, bg = _dn_proj(x, row(mix_norm_g[i]), wqkv, wbd, cw, bdp, tm=tm1)
        o = _dn_delta(qkv, bg, tb=tb, nb=DELTA_ROWS if x.shape[0] % DELTA_ROWS == 0 else 1)
        x = _mix_out(x, o, p[i], w3, w_dn_out[i].astype(BF16), w_cf_out[i].astype(BF16),
                     w_out[i].astype(BF16), w_ple_gate[i].astype(BF16), w_ple_proj[i].astype(BF16),
                     vecs, dww, dwb, tm=tm3, final=(i == depth - 1))
    return x


def kernel(x, p, mix_norm_g, w_in, dn_conv_w, dn_a_log, dn_dt_bias, dn_out_norm_g, w_dn_out,
           cf_dw_w, cf_dw_b, cf_ln_g, cf_ln_b, w_cf_out, w_out, ple_norm_g, w_ple_gate,
           w_ple_proj, final_norm_g):
    t = x.shape[1]
    return _forward(x, p, mix_norm_g, w_in, dn_conv_w, dn_a_log, dn_dt_bias, dn_out_norm_g,
                    w_dn_out, cf_dw_w, cf_dw_b, cf_ln_g, cf_ln_b, w_cf_out, w_out, ple_norm_g,
                    w_ple_gate, w_ple_proj, final_norm_g,
                    tm1=min(t, TIME_TILE), tb=min(t, TIME_TILE), tm3=min(t, TIME_TILE))
```

```python
import functools

import jax
import jax.numpy as jnp
from jax import lax
from jax.experimental import pallas as pl
from jax.experimental.pallas import tpu as pltpu

F32 = jnp.float32
BF16 = jnp.bfloat16

EPS = 1e-6
N_HEADS = 8
HEAD_DIM = 128
DN_WIDTH = N_HEADS * HEAD_DIM
DN_CONV = 4
CHUNK = 64
CF_CONV = 31
LANES = 128
SUBLANES = 8
CF_HIST = 32
CF_SLABS = SUBLANES
CF_TIME_BLOCK = 8
MIX_SUB_TILE = 256
TIME_TILE = 512
DELTA_ROWS = 4
VMEM_LIMIT = 56 * 1024 * 1024


def _sigmoid(x):
    return 0.5 * jnp.tanh(0.5 * x) + 0.5


def _silu(x):
    half = 0.5 * x
    return half * jnp.tanh(half) + half


def _rms_scale(x):
    return x * lax.rsqrt(jnp.mean(x * x, axis=-1, keepdims=True) + EPS)


def _dot(a, b):
    return jnp.dot(a, b, preferred_element_type=F32)


def _bdot(a, b):
    return lax.dot_general(a, b, (((2,), (1,)), ((0,), (0,))), preferred_element_type=F32)


def _bdot_nt(a, b):
    return lax.dot_general(a, b, (((2,), (2,)), ((0,), (0,))), preferred_element_type=F32)


def _const_spec(shape):
    nd = len(shape)
    return pl.BlockSpec(shape, lambda *_: (0,) * nd, pipeline_mode=pl.Buffered(1))


def _dn_proj_kernel(x_ref, g_ref, wqkv_ref, wbd_ref, cw_ref, bdp_ref, ltri_ref,
                    qkv_ref, bg_ref, h_ref, pz_ref, yz_ref, tail_ref, *, tm):
    t = pl.program_id(1)
    h_ref[...] = (_rms_scale(x_ref[0]) * g_ref[...]).astype(BF16)

    @pl.when(t == 0)
    def _():
        tail_ref[...] = jnp.zeros_like(tail_ref)

    hist = (DN_CONV - 1) * N_HEADS
    for sec in range(3):
        pre = _dot(h_ref[...], wqkv_ref[:, sec * DN_WIDTH:(sec + 1) * DN_WIDTH])
        pz_ref[0:hist, :] = tail_ref[sec]
        for h in range(N_HEADS):
            pz_ref[pl.ds(hist + h, tm, stride=N_HEADS), :] = pre[:, h * HEAD_DIM:(h + 1) * HEAD_DIM]
        tail_ref[sec] = pz_ref[tm * N_HEADS:tm * N_HEADS + hist, :]
        p3 = pz_ref[...].reshape(tm + DN_CONV - 1, N_HEADS, HEAD_DIM)
        acc = None
        for j in range(DN_CONV):
            term = p3[j:j + tm] * cw_ref[sec, j]
            acc = term if acc is None else acc + term
        y = _silu(acc)
        if sec < 2:
            inv = lax.rsqrt(jnp.sum(y * y, axis=-1, keepdims=True) + EPS)
            y = y * (inv * HEAD_DIM ** -0.5 if sec == 0 else inv)
        yz_ref[...] = y.reshape(tm * N_HEADS, HEAD_DIM)
        qkv_ref[0, sec] = jnp.concatenate(
            [yz_ref[pl.ds(h, tm, stride=N_HEADS), :] for h in range(N_HEADS)], axis=1).astype(BF16)

    logits = _dot(h_ref[...], wbd_ref[...])
    lane = lax.broadcasted_iota(jnp.int32, (CHUNK, LANES), 1)
    z = logits + bdp_ref[1:2, :]
    softplus = jnp.maximum(z, 0.0) + jnp.log1p(jnp.exp(-jnp.abs(z)))
    log_decay = -jnp.exp(bdp_ref[0:1, :]) * softplus
    beta = _sigmoid(logits)
    ltri = ltri_ref[...]
    for ch in range(tm // CHUNK):
        sl = slice(ch * CHUNK, (ch + 1) * CHUNK)
        seg = jnp.where(lane >= N_HEADS, log_decay[sl, :], 0.0)
        hi = seg.astype(BF16)
        r1 = seg - hi.astype(F32)
        mid = r1.astype(BF16)
        lo = (r1 - mid.astype(F32)).astype(BF16)
        g = _dot(ltri, hi) + _dot(ltri, mid) + _dot(ltri, lo)
        bg_ref[0, sl, :] = jnp.where(lane < N_HEADS, beta[sl, :], g)


def _dn_proj(x, norm_g, wqkv, wbd, cw, bdp, *, tm):
    b, t, d = x.shape
    ltri = jnp.tril(jnp.ones((CHUNK, CHUNK), F32)).astype(BF16)
    kernel = functools.partial(_dn_proj_kernel, tm=tm)
    hist = (DN_CONV - 1) * N_HEADS
    return pl.pallas_call(
        kernel,
        grid=(b, t // tm),
        in_specs=[
            pl.BlockSpec((1, tm, d), lambda i, j: (i, j, 0)),
            _const_spec((1, d)),
            _const_spec((d, 3 * DN_WIDTH)),
            _const_spec((d, LANES)),
            _const_spec((3, DN_CONV, N_HEADS, HEAD_DIM)),
            _const_spec((SUBLANES, LANES)),
            _const_spec((CHUNK, CHUNK)),
        ],
        out_specs=[
            pl.BlockSpec((1, 3, tm, DN_WIDTH), lambda i, j: (i, 0, j, 0)),
            pl.BlockSpec((1, tm, LANES), lambda i, j: (i, j, 0)),
        ],
        out_shape=[
            jax.ShapeDtypeStruct((b, 3, t, DN_WIDTH), BF16),
            jax.ShapeDtypeStruct((b, t, LANES), F32),
        ],
        scratch_shapes=[
            pltpu.VMEM((tm, d), BF16),
            pltpu.VMEM((tm * N_HEADS + hist, HEAD_DIM), F32),
            pltpu.VMEM((tm * N_HEADS, HEAD_DIM), F32),
            pltpu.VMEM((3, hist, HEAD_DIM), F32),
        ],
        compiler_params=pltpu.CompilerParams(
            dimension_semantics=("arbitrary", "arbitrary"), vmem_limit_bytes=VMEM_LIMIT),
        name="dn_proj",
    )(x, norm_g, wqkv, wbd, cw, bdp, ltri)


def _dn_delta_kernel(qkv_ref, bg_ref, o_ref, state_ref, *, tb, nb):
    @pl.when(pl.program_id(1) == 0)
    def _():
        state_ref[...] = jnp.zeros_like(state_ref)

    row = lax.broadcasted_iota(jnp.int32, (CHUNK, CHUNK), 0)
    col = lax.broadcasted_iota(jnp.int32, (CHUNK, CHUNK), 1)
    incl = row >= col
    strict = row > col
    left = lax.broadcasted_iota(jnp.int32, (CHUNK, 2 * CHUNK), 1) < CHUNK
    n_levels = CHUNK.bit_length() - 2

    def chunk(ci, carry):
        r0 = pl.multiple_of(ci * CHUNK, CHUNK)
        rows = pl.ds(r0, CHUNK)
        bg = [bg_ref[i, rows, :] for i in range(nb)]
        bg_t = [x.T for x in bg]
        pairs = [(i, h) for i in range(nb) for h in range(N_HEADS)]

        def heads(sec):
            return jnp.stack([qkv_ref[i, sec, rows, h * HEAD_DIM:(h + 1) * HEAD_DIM] for i, h in pairs])

        q, k, v = heads(0), heads(1), heads(2)
        beta = jnp.stack([bg[i][:, h:h + 1] for i, h in pairs])
        g_col = jnp.stack([bg[i][:, N_HEADS + h:N_HEADS + h + 1] for i, h in pairs])
        g_row = jnp.stack([bg_t[i][N_HEADS + h:N_HEADS + h + 1, :] for i, h in pairs])
        g_last = g_col[:, CHUNK - 1:CHUNK, :]
        decay = jnp.where(incl, jnp.exp(jnp.where(incl, g_col - g_row, 0.0)), 0.0)
        kf = k.astype(F32)
        kb = kf * beta
        eg = jnp.exp(g_col)
        kq = _bdot_nt(jnp.concatenate([kb.astype(BF16), q], axis=1), k)
        a = jnp.where(strict, kq[:, :CHUNK] * decay, 0.0)
        qk = jnp.where(incl, kq[:, CHUNK:] * decay, 0.0)
        n_heads = len(pairs)
        a4 = a.reshape(n_heads // 2, 2, CHUNK, CHUNK)
        a_pair = jnp.concatenate([a4[:, 0], a4[:, 1]], axis=2)

        def block_diag(x):
            zero = jnp.zeros_like(x)
            return jnp.concatenate([jnp.where(left, x, zero), jnp.where(left, zero, x)], axis=1)

        n = -a_pair
        ab = a_pair.astype(BF16)
        pw = _bdot(ab, block_diag(ab))
        for level in range(n_levels):
            pwd = block_diag(pw.astype(BF16))
            if level + 1 < n_levels:
                both = _bdot(jnp.concatenate([n, pw], axis=1).astype(BF16), pwd)
                n = n + pw + both[:, :CHUNK]
                pw = both[:, CHUNK:]
            else:
                n = n + pw + _bdot(n.astype(BF16), pwd)
        rhs = jnp.concatenate([v.astype(F32) * beta, kb * eg], axis=2)
        n_rhs = _bdot(block_diag(n.astype(BF16)),
                      rhs.astype(BF16).reshape(n_heads // 2, 2 * CHUNK, 2 * HEAD_DIM))
        sol = rhs + n_rhs.reshape(n_heads, CHUNK, 2 * HEAD_DIM)
        u, w = sol[:, :, :HEAD_DIM], sol[:, :, HEAD_DIM:]
        q_g = q.astype(F32) * eg
        k_dt = jnp.swapaxes(kf * jnp.exp(g_last - g_col), 1, 2)
        state = state_ref[...]
        ws = _bdot(jnp.concatenate([w, q_g], axis=1).astype(BF16), state.astype(BF16))
        vb = (u - ws[:, :CHUNK]).astype(BF16)
        ov = _bdot(jnp.concatenate([qk, k_dt], axis=1).astype(BF16), vb)
        state_ref[...] = state * jnp.exp(g_last) + ov[:, CHUNK:]
        o = (ws[:, CHUNK:] + ov[:, :CHUNK]).astype(o_ref.dtype)
        for n_pair, (i, h) in enumerate(pairs):
            o_ref[i, rows, h * HEAD_DIM:(h + 1) * HEAD_DIM] = o[n_pair]
        return carry

    lax.fori_loop(0, tb // CHUNK, chunk, 0)


def _dn_delta(qkv, bg, *, tb, nb):
    b, _, t, _ = qkv.shape
    return pl.pallas_call(
        functools.partial(_dn_delta_kernel, tb=tb, nb=nb),
        grid=(b // nb, t // tb),
        in_specs=[
            pl.BlockSpec((nb, 3, tb, DN_WIDTH), lambda i, j: (i, 0, j, 0)),
            pl.BlockSpec((nb, tb, LANES), lambda i, j: (i, j, 0)),
        ],
        out_specs=pl.BlockSpec((nb, tb, DN_WIDTH), lambda i, j: (i, j, 0)),
        out_shape=jax.ShapeDtypeStruct((b, t, DN_WIDTH), BF16),
        scratch_shapes=[pltpu.VMEM((nb * N_HEADS, HEAD_DIM, HEAD_DIM), F32)],
        compiler_params=pltpu.CompilerParams(
            dimension_semantics=("arbitrary", "arbitrary"), vmem_limit_bytes=VMEM_LIMIT),
        name="dn_delta",
    )(qkv, bg)


def _mix_out_kernel(x_ref, o_ref, p_ref, wz_ref, w5_ref, wdn_ref, wcf_ref, wout_ref, wpg_ref, wpp_ref,
                    vec_ref, dww_ref, dwb_ref, out_ref, cbuf_ref, conv_ref, *, tm, sub, rows, final):
    hist = CF_HIST * CF_SLABS

    @pl.when(pl.program_id(1) == 0)
    def _():
        cbuf_ref[0:hist, :] = jnp.zeros((hist, LANES), F32)

    mix_g, dn_g, ln_g, ln_b, ple_g, fin_g = (vec_ref[i:i + 1, :] for i in range(6))
    width = wz_ref.shape[1]

    def w5(k):
        return w5_ref[:, k * width:(k + 1) * width]

    first = CF_HIST - (CF_CONV - 1)
    bias = dwb_ref[...]

    def conv_blocks(lo, hi):
        for blk in range(lo, hi):
            base = (blk * rows + first) * CF_SLABS
            steps = [cbuf_ref[base + i * CF_SLABS:base + (i + 1) * CF_SLABS, :]
                     for i in range(rows + CF_CONV - 1)]
            acc = [bias] * rows
            for j in range(CF_CONV):
                w = dww_ref[j * CF_SLABS:(j + 1) * CF_SLABS, :]
                acc = [acc[i] + w * steps[i + j] for i in range(rows)]
            out = blk * rows * CF_SLABS
            conv_ref[out:out + rows * CF_SLABS, :] = jnp.concatenate(acc, axis=0)

    def head(r0):
        x = x_ref[0, r0:r0 + sub, :]
        h = (_rms_scale(x) * mix_g).astype(BF16)
        c = _dot(h, w5(0)) * _sigmoid(_dot(h, w5(1)))
        for s in range(CF_SLABS):
            cbuf_ref[pl.ds(hist + r0 * CF_SLABS + s, sub, stride=CF_SLABS), :] = c[:, s * LANES:(s + 1) * LANES]
        return x, h

    def tail(r0, x, h):
        rs = slice(r0, r0 + sub)
        o = o_ref[0, rs, :].astype(F32)
        normed = [_rms_scale(o[:, i * HEAD_DIM:(i + 1) * HEAD_DIM]) for i in range(N_HEADS)]
        o_n = jnp.concatenate(normed, axis=1) * dn_g
        a_in = o_n * _silu(_dot(h, wz_ref[...]))
        u_dn = _dot(a_in.astype(BF16), wdn_ref[...])
        yield
        cv = jnp.concatenate(
            [conv_ref[pl.ds(r0 * CF_SLABS + s, sub, stride=CF_SLABS), :] for s in range(CF_SLABS)], axis=1)
        cc = cv - jnp.mean(cv, axis=-1, keepdims=True)
        ln = cc * lax.rsqrt(jnp.mean(cc * cc, axis=-1, keepdims=True) + EPS) * ln_g + ln_b
        b_in = _silu(ln) * _silu(_dot(h, w5(2)))
        u_cf = _dot(b_in.astype(BF16), wcf_ref[...])
        yield
        merged = _sigmoid(_dot(h, w5(3))) * u_dn + _sigmoid(_dot(h, w5(4))) * u_cf
        x1 = x + _dot(merged.astype(BF16), wout_ref[...])
        yield
        e = _dot(p_ref[0, rs, :].astype(BF16), wpp_ref[...])
        gate = _sigmoid(_dot((_rms_scale(x1) * ple_g).astype(BF16), wpg_ref[...]))
        x2 = x1 + gate * e
        if final:
            x2 = _rms_scale(x2) * fin_g
        out_ref[0, rs, :] = x2
        yield

    n_quarters = 4
    blocks = sub // rows
    pending = None
    for r0 in range(0, tm, sub):
        x, h = head(r0)
        b0 = r0 // rows
        for q in range(n_quarters):
            if pending is not None:
                next(pending)
            conv_blocks(b0 + blocks * q // n_quarters, b0 + blocks * (q + 1) // n_quarters)
        pending = tail(r0, x, h)
    for _ in range(n_quarters):
        next(pending)
    cbuf_ref[0:hist, :] = cbuf_ref[tm * CF_SLABS:tm * CF_SLABS + hist, :]


def _mix_out(x, o, p, wz, w5, wdn, wcf, wout, wpg, wpp, vecs, dww, dwb, *, tm, final):
    b, t, d = x.shape
    assert wcf.shape[0] == CF_SLABS * LANES
    rows = CF_TIME_BLOCK
    kernel = functools.partial(_mix_out_kernel, tm=tm, sub=min(tm, MIX_SUB_TILE), rows=rows, final=final)
    tile = lambda w: pl.BlockSpec((1, tm, w), lambda i, j: (i, j, 0))
    return pl.pallas_call(
        kernel,
        grid=(b, t // tm),
        in_specs=[
            tile(d), tile(o.shape[-1]), tile(p.shape[-1]),
            _const_spec(wz.shape), _const_spec(w5.shape), _const_spec(wdn.shape), _const_spec(wcf.shape),
            _const_spec(wout.shape), _const_spec(wpg.shape), _const_spec(wpp.shape),
            _const_spec(vecs.shape), _const_spec(dww.shape), _const_spec(dwb.shape),
        ],
        out_specs=tile(d),
        out_shape=jax.ShapeDtypeStruct((b, t, d), x.dtype),
        scratch_shapes=[
            pltpu.VMEM(((tm + CF_HIST) * CF_SLABS, LANES), F32),
            pltpu.VMEM((tm * CF_SLABS, LANES), F32),
        ],
        compiler_params=pltpu.CompilerParams(
            dimension_semantics=("arbitrary", "arbitrary"), vmem_limit_bytes=VMEM_LIMIT),
        name="mix_out",
    )(x, o, p, wz, w5, wdn, wcf, wout, wpg, wpp, vecs, dww, dwb)


def _pad_rows(a, n):
    return jnp.pad(a, ((0, n - a.shape[0]), (0, 0)))


def _forward(x, p, mix_norm_g, w_in, dn_conv_w, dn_a_log, dn_dt_bias, dn_out_norm_g, w_dn_out,
             cf_dw_w, cf_dw_b, cf_ln_g, cf_ln_b, w_cf_out, w_out, ple_norm_g, w_ple_gate,
             w_ple_proj, final_norm_g, *, tm1, tb, tm3):
    depth, d, _ = w_in.shape
    cf = w_cf_out.shape[1]
    qkv_w = 3 * DN_WIDTH
    o_z, o_bd = qkv_w, qkv_w + DN_WIDTH
    o_glu = o_bd + 2 * N_HEADS
    row = lambda v: v.reshape(1, -1).astype(F32)
    for i in range(depth):
        wi = w_in[i]
        wqkv = wi[:, :qkv_w].astype(BF16)
        wbd = jnp.pad(wi[:, o_bd:o_glu], ((0, 0), (0, LANES - 2 * N_HEADS))).astype(BF16)
        cw = dn_conv_w[i].astype(F32).reshape(DN_CONV, 3, N_HEADS, HEAD_DIM).transpose(1, 0, 2, 3)
        pad_heads = lambda v: jnp.pad(v.astype(F32), (N_HEADS, LANES - 2 * N_HEADS)).reshape(1, LANES)
        bdp = _pad_rows(jnp.concatenate([pad_heads(dn_a_log[i]), pad_heads(dn_dt_bias[i])]), SUBLANES)
        assert cf == d
        wz = wi[:, o_z:o_bd].astype(BF16)
        w5 = wi[:, o_glu:].astype(BF16)
        vecs = _pad_rows(jnp.concatenate([
            row(mix_norm_g[i]), row(jnp.tile(dn_out_norm_g[i], N_HEADS)),
            row(cf_ln_g[i]), row(cf_ln_b[i]), row(ple_norm_g[i]), row(final_norm_g)]), SUBLANES)
        dww = _pad_rows(cf_dw_w[i].astype(F32), CF_HIST).reshape(CF_HIST * CF_SLABS, LANES)
        dwb = cf_dw_b[i].astype(F32).reshape(CF_SLABS, LANES)

        qkv, bg = _dn_proj(x, row(mix_norm_g[i]), wqkv, wbd, cw, bdp, tm=tm1)
        o = _dn_delta(qkv, bg, tb=tb, nb=DELTA_ROWS if x.shape[0] % DELTA_ROWS == 0 else 1)
        x = _mix_out(x, o, p[i], wz, w5, w_dn_out[i].astype(BF16), w_cf_out[i].astype(BF16),
                     w_out[i].astype(BF16), w_ple_gate[i].astype(BF16), w_ple_proj[i].astype(BF16),
                     vecs, dww, dwb, tm=tm3, final=(i == depth - 1))
    return x


def kernel(x, p, mix_norm_g, w_in, dn_conv_w, dn_a_log, dn_dt_bias, dn_out_norm_g, w_dn_out,
           cf_dw_w, cf_dw_b, cf_ln_g, cf_ln_b, w_cf_out, w_out, ple_norm_g, w_ple_gate,
           w_ple_proj, final_norm_g):
    t = x.shape[1]
    return _forward(x, p, mix_norm_g, w_in, dn_conv_w, dn_a_log, dn_dt_bias, dn_out_norm_g,
                    w_dn_out, cf_dw_w, cf_dw_b, cf_ln_g, cf_ln_b, w_cf_out, w_out, ple_norm_g,
                    w_ple_gate, w_ple_proj, final_norm_g,
                    tm1=min(t, TIME_TILE), tb=min(t, TIME_TILE), tm3=min(t, TIME_TILE))
```

```python
import functools

import jax
import jax.numpy as jnp
from jax import lax
from jax.experimental import pallas as pl
from jax.experimental.pallas import tpu as pltpu

F32 = jnp.float32
BF16 = jnp.bfloat16

EPS = 1e-6
N_HEADS = 8
HEAD_DIM = 128
DN_WIDTH = N_HEADS * HEAD_DIM
DN_CONV = 4
CHUNK = 64
CF_CONV = 31
LANES = 128
SUBLANES = 8
CF_HIST = 32
CF_SLABS = SUBLANES
CF_TIME_BLOCK = 8
MIX_SUB_TILE = 256
TIME_TILE = 512
MIX_TILE = 1024
DELTA_ROWS = 4
VMEM_LIMIT = 60 * 1024 * 1024


def _sigmoid(x):
    return 0.5 * jnp.tanh(0.5 * x) + 0.5


def _silu(x):
    half = 0.5 * x
    return half * jnp.tanh(half) + half


def _rms_scale(x):
    return x * lax.rsqrt(jnp.mean(x * x, axis=-1, keepdims=True) + EPS)


def _dot(a, b):
    return jnp.dot(a, b, preferred_element_type=F32)


def _bdot(a, b):
    return lax.dot_general(a, b, (((2,), (1,)), ((0,), (0,))), preferred_element_type=F32)


def _bdot_nt(a, b):
    return lax.dot_general(a, b, (((2,), (2,)), ((0,), (0,))), preferred_element_type=F32)


def _const_spec(shape):
    nd = len(shape)
    return pl.BlockSpec(shape, lambda *_: (0,) * nd, pipeline_mode=pl.Buffered(1))


def _dn_proj_kernel(x_ref, g_ref, wqkv_ref, wbd_ref, cw_ref, bdp_ref, ltri_ref,
                    qkv_ref, bg_ref, h_ref, pz_ref, yz_ref, tail_ref, *, tm):
    t = pl.program_id(1)
    h_ref[...] = (_rms_scale(x_ref[0]) * g_ref[...]).astype(BF16)

    @pl.when(t == 0)
    def _():
        tail_ref[...] = jnp.zeros_like(tail_ref)

    hist = (DN_CONV - 1) * N_HEADS
    for sec in range(3):
        pre = _dot(h_ref[...], wqkv_ref[:, sec * DN_WIDTH:(sec + 1) * DN_WIDTH])
        pz_ref[0:hist, :] = tail_ref[sec]
        for h in range(N_HEADS):
            pz_ref[pl.ds(hist + h, tm, stride=N_HEADS), :] = pre[:, h * HEAD_DIM:(h + 1) * HEAD_DIM]
        tail_ref[sec] = pz_ref[tm * N_HEADS:tm * N_HEADS + hist, :]
        p3 = pz_ref[...].reshape(tm + DN_CONV - 1, N_HEADS, HEAD_DIM)
        acc = None
        for j in range(DN_CONV):
            term = p3[j:j + tm] * cw_ref[sec, j]
            acc = term if acc is None else acc + term
        y = _silu(acc)
        if sec < 2:
            inv = lax.rsqrt(jnp.sum(y * y, axis=-1, keepdims=True) + EPS)
            y = y * (inv * HEAD_DIM ** -0.5 if sec == 0 else inv)
        yz_ref[...] = y.reshape(tm * N_HEADS, HEAD_DIM)
        qkv_ref[0, sec] = jnp.concatenate(
            [yz_ref[pl.ds(h, tm, stride=N_HEADS), :] for h in range(N_HEADS)], axis=1).astype(BF16)

    logits = _dot(h_ref[...], wbd_ref[...])
    lane = lax.broadcasted_iota(jnp.int32, (CHUNK, LANES), 1)
    z = logits + bdp_ref[1:2, :]
    softplus = jnp.maximum(z, 0.0) + jnp.log1p(jnp.exp(-jnp.abs(z)))
    log_decay = -jnp.exp(bdp_ref[0:1, :]) * softplus
    beta = _sigmoid(logits)
    ltri = ltri_ref[...]
    for ch in range(tm // CHUNK):
        sl = slice(ch * CHUNK, (ch + 1) * CHUNK)
        seg = jnp.where(lane >= N_HEADS, log_decay[sl, :], 0.0)
        hi = seg.astype(BF16)
        r1 = seg - hi.astype(F32)
        mid = r1.astype(BF16)
        lo = (r1 - mid.astype(F32)).astype(BF16)
        g = _dot(ltri, hi) + _dot(ltri, mid) + _dot(ltri, lo)
        bg_ref[0, sl, :] = jnp.where(lane < N_HEADS, beta[sl, :], g)


def _dn_proj(x, norm_g, wqkv, wbd, cw, bdp, *, tm):
    b, t, d = x.shape
    ltri = jnp.tril(jnp.ones((CHUNK, CHUNK), F32)).astype(BF16)
    kernel = functools.partial(_dn_proj_kernel, tm=tm)
    hist = (DN_CONV - 1) * N_HEADS
    return pl.pallas_call(
        kernel,
        grid=(b, t // tm),
        in_specs=[
            pl.BlockSpec((1, tm, d), lambda i, j: (i, j, 0)),
            _const_spec((1, d)),
            _const_spec((d, 3 * DN_WIDTH)),
            _const_spec((d, LANES)),
            _const_spec((3, DN_CONV, N_HEADS, HEAD_DIM)),
            _const_spec((SUBLANES, LANES)),
            _const_spec((CHUNK, CHUNK)),
        ],
        out_specs=[
            pl.BlockSpec((1, 3, tm, DN_WIDTH), lambda i, j: (i, 0, j, 0)),
            pl.BlockSpec((1, tm, LANES), lambda i, j: (i, j, 0)),
        ],
        out_shape=[
            jax.ShapeDtypeStruct((b, 3, t, DN_WIDTH), BF16),
            jax.ShapeDtypeStruct((b, t, LANES), F32),
        ],
        scratch_shapes=[
            pltpu.VMEM((tm, d), BF16),
            pltpu.VMEM((tm * N_HEADS + hist, HEAD_DIM), F32),
            pltpu.VMEM((tm * N_HEADS, HEAD_DIM), F32),
            pltpu.VMEM((3, hist, HEAD_DIM), F32),
        ],
        compiler_params=pltpu.CompilerParams(
            dimension_semantics=("arbitrary", "arbitrary"), vmem_limit_bytes=VMEM_LIMIT),
        name="dn_proj",
    )(x, norm_g, wqkv, wbd, cw, bdp, ltri)


def _dn_delta_kernel(qkv_ref, bg_ref, o_ref, state_ref, *, tb, nb):
    @pl.when(pl.program_id(1) == 0)
    def _():
        state_ref[...] = jnp.zeros_like(state_ref)

    row = lax.broadcasted_iota(jnp.int32, (CHUNK, CHUNK), 0)
    col = lax.broadcasted_iota(jnp.int32, (CHUNK, CHUNK), 1)
    incl = row >= col
    strict = row > col
    left = lax.broadcasted_iota(jnp.int32, (CHUNK, 2 * CHUNK), 1) < CHUNK
    n_levels = CHUNK.bit_length() - 2

    def chunk(ci, carry):
        r0 = pl.multiple_of(ci * CHUNK, CHUNK)
        rows = pl.ds(r0, CHUNK)
        bg = [bg_ref[i, rows, :] for i in range(nb)]
        bg_t = [x.T for x in bg]
        pairs = [(i, h) for i in range(nb) for h in range(N_HEADS)]

        def heads(sec):
            return jnp.stack([qkv_ref[i, sec, rows, h * HEAD_DIM:(h + 1) * HEAD_DIM] for i, h in pairs])

        q, k, v = heads(0), heads(1), heads(2)
        beta = jnp.stack([bg[i][:, h:h + 1] for i, h in pairs])
        g_col = jnp.stack([bg[i][:, N_HEADS + h:N_HEADS + h + 1] for i, h in pairs])
        g_row = jnp.stack([bg_t[i][N_HEADS + h:N_HEADS + h + 1, :] for i, h in pairs])
        g_last = g_col[:, CHUNK - 1:CHUNK, :]
        decay = jnp.where(incl, jnp.exp(jnp.where(incl, g_col - g_row, 0.0)), 0.0)
        kf = k.astype(F32)
        kb = kf * beta
        eg = jnp.exp(g_col)
        kq = _bdot_nt(jnp.concatenate([kb.astype(BF16), q], axis=1), k)
        a = jnp.where(strict, kq[:, :CHUNK] * decay, 0.0)
        qk = jnp.where(incl, kq[:, CHUNK:] * decay, 0.0)
        n_heads = len(pairs)
        a4 = a.reshape(n_heads // 2, 2, CHUNK, CHUNK)
        a_pair = jnp.concatenate([a4[:, 0], a4[:, 1]], axis=2)

        def block_diag(x):
            zero = jnp.zeros_like(x)
            return jnp.concatenate([jnp.where(left, x, zero), jnp.where(left, zero, x)], axis=1)

        n = -a_pair
        ab = a_pair.astype(BF16)
        pw = _bdot(ab, block_diag(ab))
        for level in range(n_levels):
            pwd = block_diag(pw.astype(BF16))
            if level + 1 < n_levels:
                both = _bdot(jnp.concatenate([n, pw], axis=1).astype(BF16), pwd)
                n = n + pw + both[:, :CHUNK]
                pw = both[:, CHUNK:]
            else:
                n = n + pw + _bdot(n.astype(BF16), pwd)
        rhs = jnp.concatenate([v.astype(F32) * beta, kb * eg], axis=2)
        n_rhs = _bdot(block_diag(n.astype(BF16)),
                      rhs.astype(BF16).reshape(n_heads // 2, 2 * CHUNK, 2 * HEAD_DIM))
        sol = rhs + n_rhs.reshape(n_heads, CHUNK, 2 * HEAD_DIM)
        u, w = sol[:, :, :HEAD_DIM], sol[:, :, HEAD_DIM:]
        q_g = q.astype(F32) * eg
        k_dt = jnp.swapaxes(kf * jnp.exp(g_last - g_col), 1, 2)
        state = state_ref[...]
        ws = _bdot(jnp.concatenate([w, q_g], axis=1).astype(BF16), state.astype(BF16))
        vb = (u - ws[:, :CHUNK]).astype(BF16)
        ov = _bdot(jnp.concatenate([qk, k_dt], axis=1).astype(BF16), vb)
        state_ref[...] = state * jnp.exp(g_last) + ov[:, CHUNK:]
        o = (ws[:, CHUNK:] + ov[:, :CHUNK]).astype(o_ref.dtype)
        for n_pair, (i, h) in enumerate(pairs):
            o_ref[i, rows, h * HEAD_DIM:(h + 1) * HEAD_DIM] = o[n_pair]
        return carry

    lax.fori_loop(0, tb // CHUNK, chunk, 0)


def _dn_delta(qkv, bg, *, tb, nb):
    b, _, t, _ = qkv.shape
    return pl.pallas_call(
        functools.partial(_dn_delta_kernel, tb=tb, nb=nb),
        grid=(b // nb, t // tb),
        in_specs=[
            pl.BlockSpec((nb, 3, tb, DN_WIDTH), lambda i, j: (i, 0, j, 0)),
            pl.BlockSpec((nb, tb, LANES), lambda i, j: (i, j, 0)),
        ],
        out_specs=pl.BlockSpec((nb, tb, DN_WIDTH), lambda i, j: (i, j, 0)),
        out_shape=jax.ShapeDtypeStruct((b, t, DN_WIDTH), BF16),
        scratch_shapes=[pltpu.VMEM((nb * N_HEADS, HEAD_DIM, HEAD_DIM), F32)],
        compiler_params=pltpu.CompilerParams(
            dimension_semantics=("arbitrary", "arbitrary"), vmem_limit_bytes=VMEM_LIMIT),
        name="dn_delta",
    )(qkv, bg)


def _mix_out_kernel(x_ref, o_ref, p_ref, wz_ref, w5_ref, wdn_ref, wcf_ref, wout_ref, wpg_ref, wpp_ref,
                    vec_ref, dww_ref, dwb_ref, out_ref, cbuf_ref, conv_ref, *, tm, sub, rows, final):
    hist = CF_HIST * CF_SLABS

    @pl.when(pl.program_id(1) == 0)
    def _():
        cbuf_ref[0:hist, :] = jnp.zeros((hist, LANES), F32)

    mix_g, dn_g, ln_g, ln_b, ple_g, fin_g = (vec_ref[i:i + 1, :] for i in range(6))
    width = wz_ref.shape[1]

    def w5(k):
        return w5_ref[:, k * width:(k + 1) * width]

    first = CF_HIST - (CF_CONV - 1)
    bias = dwb_ref[...]

    def conv_blocks(lo, hi):
        for blk in range(lo, hi):
            base = (blk * rows + first) * CF_SLABS
            steps = [cbuf_ref[base + i * CF_SLABS:base + (i + 1) * CF_SLABS, :]
                     for i in range(rows + CF_CONV - 1)]
            acc = [bias] * rows
            for j in range(CF_CONV):
                w = dww_ref[j * CF_SLABS:(j + 1) * CF_SLABS, :]
                acc = [acc[i] + w * steps[i + j] for i in range(rows)]
            out = blk * rows * CF_SLABS
            conv_ref[out:out + rows * CF_SLABS, :] = jnp.concatenate(acc, axis=0)

    def head(r0):
        x = x_ref[0, r0:r0 + sub, :]
        h = (_rms_scale(x) * mix_g).astype(BF16)
        c = _dot(h, w5(0)) * _sigmoid(_dot(h, w5(1)))
        for s in range(CF_SLABS):
            cbuf_ref[pl.ds(hist + r0 * CF_SLABS + s, sub, stride=CF_SLABS), :] = c[:, s * LANES:(s + 1) * LANES]
        return x, h

    def tail(r0, x, h):
        rs = slice(r0, r0 + sub)
        o = o_ref[0, rs, :].astype(F32)
        normed = [_rms_scale(o[:, i * HEAD_DIM:(i + 1) * HEAD_DIM]) for i in range(N_HEADS)]
        o_n = jnp.concatenate(normed, axis=1) * dn_g
        a_in = o_n * _silu(_dot(h, wz_ref[...]))
        u_dn = _dot(a_in.astype(BF16), wdn_ref[...])
        yield
        cv = jnp.concatenate(
            [conv_ref[pl.ds(r0 * CF_SLABS + s, sub, stride=CF_SLABS), :] for s in range(CF_SLABS)], axis=1)
        cc = cv - jnp.mean(cv, axis=-1, keepdims=True)
        ln = cc * lax.rsqrt(jnp.mean(cc * cc, axis=-1, keepdims=True) + EPS) * ln_g + ln_b
        b_in = _silu(ln) * _silu(_dot(h, w5(2)))
        u_cf = _dot(b_in.astype(BF16), wcf_ref[...])
        yield
        merged = _sigmoid(_dot(h, w5(3))) * u_dn + _sigmoid(_dot(h, w5(4))) * u_cf
        x1 = x + _dot(merged.astype(BF16), wout_ref[...])
        yield
        e = _dot(p_ref[0, rs, :].astype(BF16), wpp_ref[...])
        gate = _sigmoid(_dot((_rms_scale(x1) * ple_g).astype(BF16), wpg_ref[...]))
        x2 = x1 + gate * e
        if final:
            x2 = _rms_scale(x2) * fin_g
        out_ref[0, rs, :] = x2
        yield

    n_quarters = 4
    blocks = sub // rows
    pending = None
    for r0 in range(0, tm, sub):
        x, h = head(r0)
        b0 = r0 // rows
        for q in range(n_quarters):
            if pending is not None:
                next(pending)
            conv_blocks(b0 + blocks * q // n_quarters, b0 + blocks * (q + 1) // n_quarters)
        pending = tail(r0, x, h)
    for _ in range(n_quarters):
        next(pending)
    cbuf_ref[0:hist, :] = cbuf_ref[tm * CF_SLABS:tm * CF_SLABS + hist, :]


def _mix_out(x, o, p, wz, w5, wdn, wcf, wout, wpg, wpp, vecs, dww, dwb, *, tm, final):
    b, t, d = x.shape
    assert wcf.shape[0] == CF_SLABS * LANES
    rows = CF_TIME_BLOCK
    kernel = functools.partial(_mix_out_kernel, tm=tm, sub=min(tm, MIX_SUB_TILE), rows=rows, final=final)
    tile = lambda w, **kw: pl.BlockSpec((1, tm, w), lambda i, j: (i, j, 0), **kw)
    single = dict(pipeline_mode=pl.Buffered(1))
    return pl.pallas_call(
        kernel,
        grid=(b, t // tm),
        in_specs=[
            tile(d), tile(o.shape[-1], **single), tile(p.shape[-1], **single),
            _const_spec(wz.shape), _const_spec(w5.shape), _const_spec(wdn.shape), _const_spec(wcf.shape),
            _const_spec(wout.shape), _const_spec(wpg.shape), _const_spec(wpp.shape),
            _const_spec(vecs.shape), _const_spec(dww.shape), _const_spec(dwb.shape),
        ],
        out_specs=tile(d),
        out_shape=jax.ShapeDtypeStruct((b, t, d), x.dtype),
        scratch_shapes=[
            pltpu.VMEM(((tm + CF_HIST) * CF_SLABS, LANES), F32),
            pltpu.VMEM((tm * CF_SLABS, LANES), F32),
        ],
        compiler_params=pltpu.CompilerParams(
            dimension_semantics=("arbitrary", "arbitrary"), vmem_limit_bytes=VMEM_LIMIT),
        name="mix_out",
    )(x, o, p, wz, w5, wdn, wcf, wout, wpg, wpp, vecs, dww, dwb)


def _pad_rows(a, n):
    return jnp.pad(a, ((0, n - a.shape[0]), (0, 0)))


def _forward(x, p, mix_norm_g, w_in, dn_conv_w, dn_a_log, dn_dt_bias, dn_out_norm_g, w_dn_out,
             cf_dw_w, cf_dw_b, cf_ln_g, cf_ln_b, w_cf_out, w_out, ple_norm_g, w_ple_gate,
             w_ple_proj, final_norm_g, *, tm1, tb, tm3):
    depth, d, _ = w_in.shape
    cf = w_cf_out.shape[1]
    qkv_w = 3 * DN_WIDTH
    o_z, o_bd = qkv_w, qkv_w + DN_WIDTH
    o_glu = o_bd + 2 * N_HEADS
    row = lambda v: v.reshape(1, -1).astype(F32)
    for i in range(depth):
        wi = w_in[i]
        wqkv = wi[:, :qkv_w].astype(BF16)
        wbd = jnp.pad(wi[:, o_bd:o_glu], ((0, 0), (0, LANES - 2 * N_HEADS))).astype(BF16)
        cw = dn_conv_w[i].astype(F32).reshape(DN_CONV, 3, N_HEADS, HEAD_DIM).transpose(1, 0, 2, 3)
        pad_heads = lambda v: jnp.pad(v.astype(F32), (N_HEADS, LANES - 2 * N_HEADS)).reshape(1, LANES)
        bdp = _pad_rows(jnp.concatenate([pad_heads(dn_a_log[i]), pad_heads(dn_dt_bias[i])]), SUBLANES)
        assert cf == d
        wz = wi[:, o_z:o_bd].astype(BF16)
        w5 = wi[:, o_glu:].astype(BF16)
        vecs = _pad_rows(jnp.concatenate([
            row(mix_norm_g[i]), row(jnp.tile(dn_out_norm_g[i], N_HEADS)),
            row(cf_ln_g[i]), row(cf_ln_b[i]), row(ple_norm_g[i]), row(final_norm_g)]), SUBLANES)
        dww = _pad_rows(cf_dw_w[i].astype(F32), CF_HIST).reshape(CF_HIST * CF_SLABS, LANES)
        dwb = cf_dw_b[i].astype(F32).reshape(CF_SLABS, LANES)

        qkv, bg = _dn_proj(x, row(mix_norm_g[i]), wqkv, wbd, cw, bdp, tm=tm1)
        o = _dn_delta(qkv, bg, tb=tb, nb=DELTA_ROWS if x.shape[0] % DELTA_ROWS == 0 else 1)
        x = _mix_out(x, o, p[i], wz, w5, w_dn_out[i].astype(BF16), w_cf_out[i].astype(BF16),
                     w_out[i].astype(BF16), w_ple_gate[i].astype(BF16), w_ple_proj[i].astype(BF16),
                     vecs, dww, dwb, tm=tm3, final=(i == depth - 1))
    return x


def kernel(x, p, mix_norm_g, w_in, dn_conv_w, dn_a_log, dn_dt_bias, dn_out_norm_g, w_dn_out,
           cf_dw_w, cf_dw_b, cf_ln_g, cf_ln_b, w_cf_out, w_out, ple_norm_g, w_ple_gate,
           w_ple_proj, final_norm_g):
    t = x.shape[1]
    return _forward(x, p, mix_norm_g, w_in, dn_conv_w, dn_a_log, dn_dt_bias, dn_out_norm_g,
                    w_dn_out, cf_dw_w, cf_dw_b, cf_ln_g, cf_ln_b, w_cf_out, w_out, ple_norm_g,
                    w_ple_gate, w_ple_proj, final_norm_g,
                    tm1=min(t, TIME_TILE), tb=min(t, TIME_TILE), tm3=min(t, MIX_TILE))
```

```python
import functools

import jax
import jax.numpy as jnp
from jax import lax
from jax.experimental import pallas as pl
from jax.experimental.pallas import tpu as pltpu

F32 = jnp.float32
BF16 = jnp.bfloat16

EPS = 1e-6
N_HEADS = 8
HEAD_DIM = 128
DN_WIDTH = N_HEADS * HEAD_DIM
DN_CONV = 4
CHUNK = 64
CF_CONV = 31
LANES = 128
SUBLANES = 8
CF_HIST = 32
CF_SLABS = SUBLANES
CF_TIME_BLOCK = 8
MIX_SUB_TILE = 256
TIME_TILE = 512
DELTA_ROWS = 4
VMEM_LIMIT = 56 * 1024 * 1024


def _sigmoid(x):
    return 0.5 * jnp.tanh(0.5 * x) + 0.5


def _silu(x):
    half = 0.5 * x
    return half * jnp.tanh(half) + half


def _rms_scale(x):
    return x * lax.rsqrt(jnp.mean(x * x, axis=-1, keepdims=True) + EPS)


def _dot(a, b):
    return jnp.dot(a, b, preferred_element_type=F32)


def _bdot(a, b):
    return lax.dot_general(a, b, (((2,), (1,)), ((0,), (0,))), preferred_element_type=F32)


def _bdot_nt(a, b):
    return lax.dot_general(a, b, (((2,), (2,)), ((0,), (0,))), preferred_element_type=F32)


def _const_spec(shape):
    nd = len(shape)
    return pl.BlockSpec(shape, lambda *_: (0,) * nd, pipeline_mode=pl.Buffered(1))


def _dn_proj_kernel(x_ref, g_ref, wqkv_ref, wbd_ref, cw_ref, bdp_ref, ltri_ref,
                    qkv_ref, bg_ref, hout_ref, h_ref, pz_ref, yz_ref, tail_ref, *, tm):
    t = pl.program_id(1)
    h_ref[...] = (_rms_scale(x_ref[0]) * g_ref[...]).astype(BF16)
    hout_ref[0] = h_ref[...]

    @pl.when(t == 0)
    def _():
        tail_ref[...] = jnp.zeros_like(tail_ref)

    hist = (DN_CONV - 1) * N_HEADS
    for sec in range(3):
        pre = _dot(h_ref[...], wqkv_ref[:, sec * DN_WIDTH:(sec + 1) * DN_WIDTH])
        pz_ref[0:hist, :] = tail_ref[sec]
        for h in range(N_HEADS):
            pz_ref[pl.ds(hist + h, tm, stride=N_HEADS), :] = pre[:, h * HEAD_DIM:(h + 1) * HEAD_DIM]
        tail_ref[sec] = pz_ref[tm * N_HEADS:tm * N_HEADS + hist, :]
        p3 = pz_ref[...].reshape(tm + DN_CONV - 1, N_HEADS, HEAD_DIM)
        acc = None
        for j in range(DN_CONV):
            term = p3[j:j + tm] * cw_ref[sec, j]
            acc = term if acc is None else acc + term
        y = _silu(acc)
        if sec < 2:
            inv = lax.rsqrt(jnp.sum(y * y, axis=-1, keepdims=True) + EPS)
            y = y * (inv * HEAD_DIM ** -0.5 if sec == 0 else inv)
        yz_ref[...] = y.reshape(tm * N_HEADS, HEAD_DIM)
        qkv_ref[0, sec] = jnp.concatenate(
            [yz_ref[pl.ds(h, tm, stride=N_HEADS), :] for h in range(N_HEADS)], axis=1).astype(BF16)

    logits = _dot(h_ref[...], wbd_ref[...])
    lane = lax.broadcasted_iota(jnp.int32, (CHUNK, LANES), 1)
    z = logits + bdp_ref[1:2, :]
    softplus = jnp.maximum(z, 0.0) + jnp.log1p(jnp.exp(-jnp.abs(z)))
    log_decay = -jnp.exp(bdp_ref[0:1, :]) * softplus
    beta = _sigmoid(logits)
    ltri = ltri_ref[...]
    for ch in range(tm // CHUNK):
        sl = slice(ch * CHUNK, (ch + 1) * CHUNK)
        seg = jnp.where(lane >= N_HEADS, log_decay[sl, :], 0.0)
        hi = seg.astype(BF16)
        r1 = seg - hi.astype(F32)
        mid = r1.astype(BF16)
        lo = (r1 - mid.astype(F32)).astype(BF16)
        g = _dot(ltri, hi) + _dot(ltri, mid) + _dot(ltri, lo)
        bg_ref[0, sl, :] = jnp.where(lane < N_HEADS, beta[sl, :], g)


def _dn_proj(x, norm_g, wqkv, wbd, cw, bdp, *, tm):
    b, t, d = x.shape
    ltri = jnp.tril(jnp.ones((CHUNK, CHUNK), F32)).astype(BF16)
    kernel = functools.partial(_dn_proj_kernel, tm=tm)
    hist = (DN_CONV - 1) * N_HEADS
    return pl.pallas_call(
        kernel,
        grid=(b, t // tm),
        in_specs=[
            pl.BlockSpec((1, tm, d), lambda i, j: (i, j, 0)),
            _const_spec((1, d)),
            _const_spec((d, 3 * DN_WIDTH)),
            _const_spec((d, LANES)),
            _const_spec((3, DN_CONV, N_HEADS, HEAD_DIM)),
            _const_spec((SUBLANES, LANES)),
            _const_spec((CHUNK, CHUNK)),
        ],
        out_specs=[
            pl.BlockSpec((1, 3, tm, DN_WIDTH), lambda i, j: (i, 0, j, 0)),
            pl.BlockSpec((1, tm, LANES), lambda i, j: (i, j, 0)),
            pl.BlockSpec((1, tm, d), lambda i, j: (i, j, 0)),
        ],
        out_shape=[
            jax.ShapeDtypeStruct((b, 3, t, DN_WIDTH), BF16),
            jax.ShapeDtypeStruct((b, t, LANES), F32),
            jax.ShapeDtypeStruct((b, t, d), BF16),
        ],
        scratch_shapes=[
            pltpu.VMEM((tm, d), BF16),
            pltpu.VMEM((tm * N_HEADS + hist, HEAD_DIM), F32),
            pltpu.VMEM((tm * N_HEADS, HEAD_DIM), F32),
            pltpu.VMEM((3, hist, HEAD_DIM), F32),
        ],
        compiler_params=pltpu.CompilerParams(
            dimension_semantics=("arbitrary", "arbitrary"), vmem_limit_bytes=VMEM_LIMIT),
        name="dn_proj",
    )(x, norm_g, wqkv, wbd, cw, bdp, ltri)


def _dn_delta_kernel(qkv_ref, bg_ref, o_ref, state_ref, *, tb, nb):
    @pl.when(pl.program_id(1) == 0)
    def _():
        state_ref[...] = jnp.zeros_like(state_ref)

    row = lax.broadcasted_iota(jnp.int32, (CHUNK, CHUNK), 0)
    col = lax.broadcasted_iota(jnp.int32, (CHUNK, CHUNK), 1)
    incl = row >= col
    strict = row > col
    left = lax.broadcasted_iota(jnp.int32, (CHUNK, 2 * CHUNK), 1) < CHUNK
    n_levels = CHUNK.bit_length() - 2

    def chunk(ci, carry):
        r0 = pl.multiple_of(ci * CHUNK, CHUNK)
        rows = pl.ds(r0, CHUNK)
        bg = [bg_ref[i, rows, :] for i in range(nb)]
        bg_t = [x.T for x in bg]
        pairs = [(i, h) for i in range(nb) for h in range(N_HEADS)]

        def heads(sec):
            return jnp.stack([qkv_ref[i, sec, rows, h * HEAD_DIM:(h + 1) * HEAD_DIM] for i, h in pairs])

        q, k, v = heads(0), heads(1), heads(2)
        beta = jnp.stack([bg[i][:, h:h + 1] for i, h in pairs])
        g_col = jnp.stack([bg[i][:, N_HEADS + h:N_HEADS + h + 1] for i, h in pairs])
        g_row = jnp.stack([bg_t[i][N_HEADS + h:N_HEADS + h + 1, :] for i, h in pairs])
        g_last = g_col[:, CHUNK - 1:CHUNK, :]
        decay = jnp.where(incl, jnp.exp(jnp.where(incl, g_col - g_row, 0.0)), 0.0)
        kf = k.astype(F32)
        kb = kf * beta
        eg = jnp.exp(g_col)
        kq = _bdot_nt(jnp.concatenate([kb.astype(BF16), q], axis=1), k)
        a = jnp.where(strict, kq[:, :CHUNK] * decay, 0.0)
        qk = jnp.where(incl, kq[:, CHUNK:] * decay, 0.0)
        n_heads = len(pairs)
        a4 = a.reshape(n_heads // 2, 2, CHUNK, CHUNK)
        a_pair = jnp.concatenate([a4[:, 0], a4[:, 1]], axis=2)

        def block_diag(x):
            zero = jnp.zeros_like(x)
            return jnp.concatenate([jnp.where(left, x, zero), jnp.where(left, zero, x)], axis=1)

        n = -a_pair
        ab = a_pair.astype(BF16)
        pw = _bdot(ab, block_diag(ab))
        for level in range(n_levels):
            pwd = block_diag(pw.astype(BF16))
            if level + 1 < n_levels:
                both = _bdot(jnp.concatenate([n, pw], axis=1).astype(BF16), pwd)
                n = n + pw + both[:, :CHUNK]
                pw = both[:, CHUNK:]
            else:
                n = n + pw + _bdot(n.astype(BF16), pwd)
        rhs = jnp.concatenate([v.astype(F32) * beta, kb * eg], axis=2)
        n_rhs = _bdot(block_diag(n.astype(BF16)),
                      rhs.astype(BF16).reshape(n_heads // 2, 2 * CHUNK, 2 * HEAD_DIM))
        sol = rhs + n_rhs.reshape(n_heads, CHUNK, 2 * HEAD_DIM)
        u, w = sol[:, :, :HEAD_DIM], sol[:, :, HEAD_DIM:]
        q_g = q.astype(F32) * eg
        k_dt = jnp.swapaxes(kf * jnp.exp(g_last - g_col), 1, 2)
        state = state_ref[...]
        ws = _bdot(jnp.concatenate([w, q_g], axis=1).astype(BF16), state.astype(BF16))
        vb = (u - ws[:, :CHUNK]).astype(BF16)
        ov = _bdot(jnp.concatenate([qk, k_dt], axis=1).astype(BF16), vb)
        state_ref[...] = state * jnp.exp(g_last) + ov[:, CHUNK:]
        o = (ws[:, CHUNK:] + ov[:, :CHUNK]).astype(o_ref.dtype)
        for n_pair, (i, h) in enumerate(pairs):
            o_ref[i, rows, h * HEAD_DIM:(h + 1) * HEAD_DIM] = o[n_pair]
        return carry

    lax.fori_loop(0, tb // CHUNK, chunk, 0)


def _dn_delta(qkv, bg, *, tb, nb):
    b, _, t, _ = qkv.shape
    return pl.pallas_call(
        functools.partial(_dn_delta_kernel, tb=tb, nb=nb),
        grid=(b // nb, t // tb),
        in_specs=[
            pl.BlockSpec((nb, 3, tb, DN_WIDTH), lambda i, j: (i, 0, j, 0)),
            pl.BlockSpec((nb, tb, LANES), lambda i, j: (i, j, 0)),
        ],
        out_specs=pl.BlockSpec((nb, tb, DN_WIDTH), lambda i, j: (i, j, 0)),
        out_shape=jax.ShapeDtypeStruct((b, t, DN_WIDTH), BF16),
        scratch_shapes=[pltpu.VMEM((nb * N_HEADS, HEAD_DIM, HEAD_DIM), F32)],
        compiler_params=pltpu.CompilerParams(
            dimension_semantics=("arbitrary", "arbitrary"), vmem_limit_bytes=VMEM_LIMIT),
        name="dn_delta",
    )(qkv, bg)


def _mix_out_kernel(x_ref, hm_ref, o_ref, p_ref, wz_ref, w5_ref, wdn_ref, wcf_ref, wout_ref, wpg_ref, wpp_ref,
                    vec_ref, dww_ref, dwb_ref, out_ref, cbuf_ref, conv_ref, *, tm, sub, rows, final):
    hist = CF_HIST * CF_SLABS

    @pl.when(pl.program_id(1) == 0)
    def _():
        cbuf_ref[0:hist, :] = jnp.zeros((hist, LANES), F32)

    dn_g, ln_g, ln_b, ple_g, fin_g = (vec_ref[i:i + 1, :] for i in range(5))
    width = wz_ref.shape[1]

    def w5(k):
        return w5_ref[:, k * width:(k + 1) * width]

    first = CF_HIST - (CF_CONV - 1)
    bias = dwb_ref[...]

    def conv_blocks(lo, hi):
        for blk in range(lo, hi):
            base = (blk * rows + first) * CF_SLABS
            steps = [cbuf_ref[base + i * CF_SLABS:base + (i + 1) * CF_SLABS, :]
                     for i in range(rows + CF_CONV - 1)]
            acc = [bias] * rows
            for j in range(CF_CONV):
                w = dww_ref[j * CF_SLABS:(j + 1) * CF_SLABS, :]
                acc = [acc[i] + w * steps[i + j] for i in range(rows)]
            out = blk * rows * CF_SLABS
            conv_ref[out:out + rows * CF_SLABS, :] = jnp.concatenate(acc, axis=0)

    def head(r0):
        x = x_ref[0, r0:r0 + sub, :]
        h = hm_ref[0, r0:r0 + sub, :]
        c = _dot(h, w5(0)) * _sigmoid(_dot(h, w5(1)))
        for s in range(CF_SLABS):
            cbuf_ref[pl.ds(hist + r0 * CF_SLABS + s, sub, stride=CF_SLABS), :] = c[:, s * LANES:(s + 1) * LANES]
        return x, h

    def tail(r0, x, h):
        rs = slice(r0, r0 + sub)
        o = o_ref[0, rs, :].astype(F32)
        normed = [_rms_scale(o[:, i * HEAD_DIM:(i + 1) * HEAD_DIM]) for i in range(N_HEADS)]
        o_n = jnp.concatenate(normed, axis=1) * dn_g
        a_in = o_n * _silu(_dot(h, wz_ref[...]))
        u_dn = _dot(a_in.astype(BF16), wdn_ref[...])
        yield
        cv = jnp.concatenate(
            [conv_ref[pl.ds(r0 * CF_SLABS + s, sub, stride=CF_SLABS), :] for s in range(CF_SLABS)], axis=1)
        cc = cv - jnp.mean(cv, axis=-1, keepdims=True)
        ln = cc * lax.rsqrt(jnp.mean(cc * cc, axis=-1, keepdims=True) + EPS) * ln_g + ln_b
        b_in = _silu(ln) * _silu(_dot(h, w5(2)))
        u_cf = _dot(b_in.astype(BF16), wcf_ref[...])
        yield
        merged = _sigmoid(_dot(h, w5(3))) * u_dn + _sigmoid(_dot(h, w5(4))) * u_cf
        x1 = x + _dot(merged.astype(BF16), wout_ref[...])
        yield
        e = _dot(p_ref[0, rs, :].astype(BF16), wpp_ref[...])
        gate = _sigmoid(_dot((_rms_scale(x1) * ple_g).astype(BF16), wpg_ref[...]))
        x2 = x1 + gate * e
        if final:
            x2 = _rms_scale(x2) * fin_g
        out_ref[0, rs, :] = x2
        yield

    n_quarters = 4
    blocks = sub // rows
    pending = None
    for r0 in range(0, tm, sub):
        x, h = head(r0)
        b0 = r0 // rows
        for q in range(n_quarters):
            if pending is not None:
                next(pending)
            conv_blocks(b0 + blocks * q // n_quarters, b0 + blocks * (q + 1) // n_quarters)
        pending = tail(r0, x, h)
    for _ in range(n_quarters):
        next(pending)
    cbuf_ref[0:hist, :] = cbuf_ref[tm * CF_SLABS:tm * CF_SLABS + hist, :]


def _mix_out(x, hm, o, p, wz, w5, wdn, wcf, wout, wpg, wpp, vecs, dww, dwb, *, tm, final):
    b, t, d = x.shape
    assert wcf.shape[0] == CF_SLABS * LANES
    rows = CF_TIME_BLOCK
    kernel = functools.partial(_mix_out_kernel, tm=tm, sub=min(tm, MIX_SUB_TILE), rows=rows, final=final)
    tile = lambda w: pl.BlockSpec((1, tm, w), lambda i, j: (i, j, 0))
    return pl.pallas_call(
        kernel,
        grid=(b, t // tm),
        in_specs=[
            tile(d), tile(d), tile(o.shape[-1]), tile(p.shape[-1]),
            _const_spec(wz.shape), _const_spec(w5.shape), _const_spec(wdn.shape), _const_spec(wcf.shape),
            _const_spec(wout.shape), _const_spec(wpg.shape), _const_spec(wpp.shape),
            _const_spec(vecs.shape), _const_spec(dww.shape), _const_spec(dwb.shape),
        ],
        out_specs=tile(d),
        out_shape=jax.ShapeDtypeStruct((b, t, d), x.dtype),
        scratch_shapes=[
            pltpu.VMEM(((tm + CF_HIST) * CF_SLABS, LANES), F32),
            pltpu.VMEM((tm * CF_SLABS, LANES), F32),
        ],
        compiler_params=pltpu.CompilerParams(
            dimension_semantics=("arbitrary", "arbitrary"), vmem_limit_bytes=VMEM_LIMIT),
        name="mix_out",
    )(x, hm, o, p, wz, w5, wdn, wcf, wout, wpg, wpp, vecs, dww, dwb)


def _pad_rows(a, n):
    return jnp.pad(a, ((0, n - a.shape[0]), (0, 0)))


def _forward(x, p, mix_norm_g, w_in, dn_conv_w, dn_a_log, dn_dt_bias, dn_out_norm_g, w_dn_out,
             cf_dw_w, cf_dw_b, cf_ln_g, cf_ln_b, w_cf_out, w_out, ple_norm_g, w_ple_gate,
             w_ple_proj, final_norm_g, *, tm1, tb, tm3):
    depth, d, _ = w_in.shape
    cf = w_cf_out.shape[1]
    qkv_w = 3 * DN_WIDTH
    o_z, o_bd = qkv_w, qkv_w + DN_WIDTH
    o_glu = o_bd + 2 * N_HEADS
    row = lambda v: v.reshape(1, -1).astype(F32)
    for i in range(depth):
        wi = w_in[i]
        wqkv = wi[:, :qkv_w].astype(BF16)
        wbd = jnp.pad(wi[:, o_bd:o_glu], ((0, 0), (0, LANES - 2 * N_HEADS))).astype(BF16)
        cw = dn_conv_w[i].astype(F32).reshape(DN_CONV, 3, N_HEADS, HEAD_DIM).transpose(1, 0, 2, 3)
        pad_heads = lambda v: jnp.pad(v.astype(F32), (N_HEADS, LANES - 2 * N_HEADS)).reshape(1, LANES)
        bdp = _pad_rows(jnp.concatenate([pad_heads(dn_a_log[i]), pad_heads(dn_dt_bias[i])]), SUBLANES)
        assert cf == d
        wz = wi[:, o_z:o_bd].astype(BF16)
        w5 = wi[:, o_glu:].astype(BF16)
        vecs = _pad_rows(jnp.concatenate([
            row(jnp.tile(dn_out_norm_g[i], N_HEADS)),
            row(cf_ln_g[i]), row(cf_ln_b[i]), row(ple_norm_g[i]), row(final_norm_g)]), SUBLANES)
        dww = _pad_rows(cf_dw_w[i].astype(F32), CF_HIST).reshape(CF_HIST * CF_SLABS, LANES)
        dwb = cf_dw_b[i].astype(F32).reshape(CF_SLABS, LANES)

        qkv, bg, hm = _dn_proj(x, row(mix_norm_g[i]), wqkv, wbd, cw, bdp, tm=tm1)
        o = _dn_delta(qkv, bg, tb=tb, nb=DELTA_ROWS if x.shape[0] % DELTA_ROWS == 0 else 1)
        x = _mix_out(x, hm, o, p[i], wz, w5, w_dn_out[i].astype(BF16), w_cf_out[i].astype(BF16),
                     w_out[i].astype(BF16), w_ple_gate[i].astype(BF16), w_ple_proj[i].astype(BF16),
                     vecs, dww, dwb, tm=tm3, final=(i == depth - 1))
    return x


def kernel(x, p, mix_norm_g, w_in, dn_conv_w, dn_a_log, dn_dt_bias, dn_out_norm_g, w_dn_out,
           cf_dw_w, cf_dw_b, cf_ln_g, cf_ln_b, w_cf_out, w_out, ple_norm_g, w_ple_gate,
           w_ple_proj, final_norm_g):
    t = x.shape[1]
    return _forward(x, p, mix_norm_g, w_in, dn_conv_w, dn_a_log, dn_dt_bias, dn_out_norm_g,
                    w_dn_out, cf_dw_w, cf_dw_b, cf_ln_g, cf_ln_b, w_cf_out, w_out, ple_norm_g,
                    w_ple_gate, w_ple_proj, final_norm_g,
                    tm1=min(t, TIME_TILE), tb=min(t, TIME_TILE), tm3=min(t, TIME_TILE))
```

```python
import functools

import jax
import jax.numpy as jnp
from jax import lax
from jax.experimental import pallas as pl
from jax.experimental.pallas import tpu as pltpu

F32 = jnp.float32
BF16 = jnp.bfloat16

EPS = 1e-6
N_HEADS = 8
HEAD_DIM = 128
DN_WIDTH = N_HEADS * HEAD_DIM
DN_CONV = 4
CHUNK = 64
CF_CONV = 31
LANES = 128
SUBLANES = 8
CF_HIST = 32
CF_SLABS = SUBLANES
CF_TIME_BLOCK = 8
MIX_SUB_TILE = 256
TIME_TILE = 512
DELTA_ROWS = 4
VMEM_LIMIT = 56 * 1024 * 1024


def _sigmoid(x):
    return 0.5 * jnp.tanh(0.5 * x) + 0.5


def _silu(x):
    half = 0.5 * x
    return half * jnp.tanh(half) + half


def _rms_scale(x):
    return x * lax.rsqrt(jnp.mean(x * x, axis=-1, keepdims=True) + EPS)


def _dot(a, b):
    return jnp.dot(a, b, preferred_element_type=F32)


def _bdot(a, b):
    return lax.dot_general(a, b, (((2,), (1,)), ((0,), (0,))), preferred_element_type=F32)


def _bdot_nt(a, b):
    return lax.dot_general(a, b, (((2,), (2,)), ((0,), (0,))), preferred_element_type=F32)


def _const_spec(shape):
    nd = len(shape)
    return pl.BlockSpec(shape, lambda *_: (0,) * nd, pipeline_mode=pl.Buffered(1))


def _dn_proj_kernel(x_ref, g_ref, wqkv_ref, wbd_ref, cw_ref, bdp_ref, ltri_ref,
                    qkv_ref, bg_ref, hout_ref, h_ref, pz_ref, yz_ref, tail_ref, *, tm):
    t = pl.program_id(1)
    h_ref[...] = (_rms_scale(x_ref[0]) * g_ref[...]).astype(BF16)
    hout_ref[0] = h_ref[...]

    @pl.when(t == 0)
    def _():
        tail_ref[...] = jnp.zeros_like(tail_ref)

    hist = (DN_CONV - 1) * N_HEADS
    for sec in range(3):
        pre = _dot(h_ref[...], wqkv_ref[:, sec * DN_WIDTH:(sec + 1) * DN_WIDTH])
        pz_ref[0:hist, :] = tail_ref[sec]
        for h in range(N_HEADS):
            pz_ref[pl.ds(hist + h, tm, stride=N_HEADS), :] = pre[:, h * HEAD_DIM:(h + 1) * HEAD_DIM]
        tail_ref[sec] = pz_ref[tm * N_HEADS:tm * N_HEADS + hist, :]
        p3 = pz_ref[...].reshape(tm + DN_CONV - 1, N_HEADS, HEAD_DIM)
        acc = None
        for j in range(DN_CONV):
            term = p3[j:j + tm] * cw_ref[sec, j]
            acc = term if acc is None else acc + term
        y = _silu(acc)
        if sec < 2:
            inv = lax.rsqrt(jnp.sum(y * y, axis=-1, keepdims=True) + EPS)
            y = y * (inv * HEAD_DIM ** -0.5 if sec == 0 else inv)
        yz_ref[...] = y.reshape(tm * N_HEADS, HEAD_DIM)
        qkv_ref[0, sec] = jnp.concatenate(
            [yz_ref[pl.ds(h, tm, stride=N_HEADS), :] for h in range(N_HEADS)], axis=1).astype(BF16)

    logits = _dot(h_ref[...], wbd_ref[...])
    lane = lax.broadcasted_iota(jnp.int32, (CHUNK, LANES), 1)
    z = logits + bdp_ref[1:2, :]
    softplus = jnp.maximum(z, 0.0) + jnp.log1p(jnp.exp(-jnp.abs(z)))
    log_decay = -jnp.exp(bdp_ref[0:1, :]) * softplus
    beta = _sigmoid(logits)
    ltri = ltri_ref[...]
    for ch in range(tm // CHUNK):
        sl = slice(ch * CHUNK, (ch + 1) * CHUNK)
        seg = jnp.where(lane >= N_HEADS, log_decay[sl, :], 0.0)
        hi = seg.astype(BF16)
        r1 = seg - hi.astype(F32)
        mid = r1.astype(BF16)
        lo = (r1 - mid.astype(F32)).astype(BF16)
        g = _dot(ltri, hi) + _dot(ltri, mid) + _dot(ltri, lo)
        bg_ref[0, sl, :] = jnp.where(lane < N_HEADS, beta[sl, :], g)


def _dn_proj(x, norm_g, wqkv, wbd, cw, bdp, *, tm):
    b, t, d = x.shape
    ltri = jnp.tril(jnp.ones((CHUNK, CHUNK), F32)).astype(BF16)
    kernel = functools.partial(_dn_proj_kernel, tm=tm)
    hist = (DN_CONV - 1) * N_HEADS
    return pl.pallas_call(
        kernel,
        grid=(b, t // tm),
        in_specs=[
            pl.BlockSpec((1, tm, d), lambda i, j: (i, j, 0)),
            _const_spec((1, d)),
            _const_spec((d, 3 * DN_WIDTH)),
            _const_spec((d, LANES)),
            _const_spec((3, DN_CONV, N_HEADS, HEAD_DIM)),
            _const_spec((SUBLANES, LANES)),
            _const_spec((CHUNK, CHUNK)),
        ],
        out_specs=[
            pl.BlockSpec((1, 3, tm, DN_WIDTH), lambda i, j: (i, 0, j, 0)),
            pl.BlockSpec((1, tm, LANES), lambda i, j: (i, j, 0)),
            pl.BlockSpec((1, tm, d), lambda i, j: (i, j, 0)),
        ],
        out_shape=[
            jax.ShapeDtypeStruct((b, 3, t, DN_WIDTH), BF16),
            jax.ShapeDtypeStruct((b, t, LANES), F32),
            jax.ShapeDtypeStruct((b, t, d), BF16),
        ],
        scratch_shapes=[
            pltpu.VMEM((tm, d), BF16),
            pltpu.VMEM((tm * N_HEADS + hist, HEAD_DIM), F32),
            pltpu.VMEM((tm * N_HEADS, HEAD_DIM), F32),
            pltpu.VMEM((3, hist, HEAD_DIM), F32),
        ],
        compiler_params=pltpu.CompilerParams(
            dimension_semantics=("arbitrary", "arbitrary"), vmem_limit_bytes=VMEM_LIMIT),
        name="dn_proj",
    )(x, norm_g, wqkv, wbd, cw, bdp, ltri)


def _dn_delta_kernel(qkv_ref, bg_ref, og_ref, o_ref, state_ref, *, tb, nb):
    @pl.when(pl.program_id(1) == 0)
    def _():
        state_ref[...] = jnp.zeros_like(state_ref)

    row = lax.broadcasted_iota(jnp.int32, (CHUNK, CHUNK), 0)
    col = lax.broadcasted_iota(jnp.int32, (CHUNK, CHUNK), 1)
    incl = row >= col
    strict = row > col
    left = lax.broadcasted_iota(jnp.int32, (CHUNK, 2 * CHUNK), 1) < CHUNK
    n_levels = CHUNK.bit_length() - 2

    def chunk(ci, carry):
        r0 = pl.multiple_of(ci * CHUNK, CHUNK)
        rows = pl.ds(r0, CHUNK)
        bg = [bg_ref[i, rows, :] for i in range(nb)]
        bg_t = [x.T for x in bg]
        pairs = [(i, h) for i in range(nb) for h in range(N_HEADS)]

        def heads(sec):
            return jnp.stack([qkv_ref[i, sec, rows, h * HEAD_DIM:(h + 1) * HEAD_DIM] for i, h in pairs])

        q, k, v = heads(0), heads(1), heads(2)
        beta = jnp.stack([bg[i][:, h:h + 1] for i, h in pairs])
        g_col = jnp.stack([bg[i][:, N_HEADS + h:N_HEADS + h + 1] for i, h in pairs])
        g_row = jnp.stack([bg_t[i][N_HEADS + h:N_HEADS + h + 1, :] for i, h in pairs])
        g_last = g_col[:, CHUNK - 1:CHUNK, :]
        decay = jnp.where(incl, jnp.exp(jnp.where(incl, g_col - g_row, 0.0)), 0.0)
        kf = k.astype(F32)
        kb = kf * beta
        eg = jnp.exp(g_col)
        kq = _bdot_nt(jnp.concatenate([kb.astype(BF16), q], axis=1), k)
        a = jnp.where(strict, kq[:, :CHUNK] * decay, 0.0)
        qk = jnp.where(incl, kq[:, CHUNK:] * decay, 0.0)
        n_heads = len(pairs)
        a4 = a.reshape(n_heads // 2, 2, CHUNK, CHUNK)
        a_pair = jnp.concatenate([a4[:, 0], a4[:, 1]], axis=2)

        def block_diag(x):
            zero = jnp.zeros_like(x)
            return jnp.concatenate([jnp.where(left, x, zero), jnp.where(left, zero, x)], axis=1)

        n = -a_pair
        ab = a_pair.astype(BF16)
        pw = _bdot(ab, block_diag(ab))
        for level in range(n_levels):
            pwd = block_diag(pw.astype(BF16))
            if level + 1 < n_levels:
                both = _bdot(jnp.concatenate([n, pw], axis=1).astype(BF16), pwd)
                n = n + pw + both[:, :CHUNK]
                pw = both[:, CHUNK:]
            else:
                n = n + pw + _bdot(n.astype(BF16), pwd)
        rhs = jnp.concatenate([v.astype(F32) * beta, kb * eg], axis=2)
        n_rhs = _bdot(block_diag(n.astype(BF16)),
                      rhs.astype(BF16).reshape(n_heads // 2, 2 * CHUNK, 2 * HEAD_DIM))
        sol = rhs + n_rhs.reshape(n_heads, CHUNK, 2 * HEAD_DIM)
        u, w = sol[:, :, :HEAD_DIM], sol[:, :, HEAD_DIM:]
        q_g = q.astype(F32) * eg
        k_dt = jnp.swapaxes(kf * jnp.exp(g_last - g_col), 1, 2)
        state = state_ref[...]
        ws = _bdot(jnp.concatenate([w, q_g], axis=1).astype(BF16), state.astype(BF16))
        vb = (u - ws[:, :CHUNK]).astype(BF16)
        ov = _bdot(jnp.concatenate([qk, k_dt], axis=1).astype(BF16), vb)
        state_ref[...] = state * jnp.exp(g_last) + ov[:, CHUNK:]
        o = (_rms_scale(ws[:, CHUNK:] + ov[:, :CHUNK]) * og_ref[...]).astype(o_ref.dtype)
        for n_pair, (i, h) in enumerate(pairs):
            o_ref[i, rows, h * HEAD_DIM:(h + 1) * HEAD_DIM] = o[n_pair]
        return carry

    lax.fori_loop(0, tb // CHUNK, chunk, 0)


def _dn_delta(qkv, bg, og, *, tb, nb):
    b, _, t, _ = qkv.shape
    return pl.pallas_call(
        functools.partial(_dn_delta_kernel, tb=tb, nb=nb),
        grid=(b // nb, t // tb),
        in_specs=[
            pl.BlockSpec((nb, 3, tb, DN_WIDTH), lambda i, j: (i, 0, j, 0)),
            pl.BlockSpec((nb, tb, LANES), lambda i, j: (i, j, 0)),
            _const_spec(og.shape),
        ],
        out_specs=pl.BlockSpec((nb, tb, DN_WIDTH), lambda i, j: (i, j, 0)),
        out_shape=jax.ShapeDtypeStruct((b, t, DN_WIDTH), BF16),
        scratch_shapes=[pltpu.VMEM((nb * N_HEADS, HEAD_DIM, HEAD_DIM), F32)],
        compiler_params=pltpu.CompilerParams(
            dimension_semantics=("arbitrary", "arbitrary"), vmem_limit_bytes=VMEM_LIMIT),
        name="dn_delta",
    )(qkv, bg, og)


def _mix_out_kernel(x_ref, hm_ref, o_ref, p_ref, wz_ref, w5_ref, wdn_ref, wcf_ref, wout_ref, wpg_ref, wpp_ref,
                    vec_ref, dww_ref, dwb_ref, out_ref, cbuf_ref, conv_ref, *, tm, sub, rows, final):
    hist = CF_HIST * CF_SLABS

    @pl.when(pl.program_id(1) == 0)
    def _():
        cbuf_ref[0:hist, :] = jnp.zeros((hist, LANES), F32)

    ln_g, ln_b, ple_g, fin_g = (vec_ref[i:i + 1, :] for i in range(4))
    width = wz_ref.shape[1]

    def w5(k):
        return w5_ref[:, k * width:(k + 1) * width]

    first = CF_HIST - (CF_CONV - 1)
    bias = dwb_ref[...]

    def conv_blocks(lo, hi):
        for blk in range(lo, hi):
            base = (blk * rows + first) * CF_SLABS
            steps = [cbuf_ref[base + i * CF_SLABS:base + (i + 1) * CF_SLABS, :]
                     for i in range(rows + CF_CONV - 1)]
            acc = [bias] * rows
            for j in range(CF_CONV):
                w = dww_ref[j * CF_SLABS:(j + 1) * CF_SLABS, :]
                acc = [acc[i] + w * steps[i + j] for i in range(rows)]
            out = blk * rows * CF_SLABS
            conv_ref[out:out + rows * CF_SLABS, :] = jnp.concatenate(acc, axis=0)

    def head(r0):
        x = x_ref[0, r0:r0 + sub, :]
        h = hm_ref[0, r0:r0 + sub, :]
        c = _dot(h, w5(0)) * _sigmoid(_dot(h, w5(1)))
        for s in range(CF_SLABS):
            cbuf_ref[pl.ds(hist + r0 * CF_SLABS + s, sub, stride=CF_SLABS), :] = c[:, s * LANES:(s + 1) * LANES]
        return x, h

    def tail(r0, x, h):
        rs = slice(r0, r0 + sub)
        a_in = o_ref[0, rs, :].astype(F32) * _silu(_dot(h, wz_ref[...]))
        u_dn = _dot(a_in.astype(BF16), wdn_ref[...])
        yield
        cv = jnp.concatenate(
            [conv_ref[pl.ds(r0 * CF_SLABS + s, sub, stride=CF_SLABS), :] for s in range(CF_SLABS)], axis=1)
        cc = cv - jnp.mean(cv, axis=-1, keepdims=True)
        ln = cc * lax.rsqrt(jnp.mean(cc * cc, axis=-1, keepdims=True) + EPS) * ln_g + ln_b
        b_in = _silu(ln) * _silu(_dot(h, w5(2)))
        u_cf = _dot(b_in.astype(BF16), wcf_ref[...])
        yield
        merged = _sigmoid(_dot(h, w5(3))) * u_dn + _sigmoid(_dot(h, w5(4))) * u_cf
        x1 = x + _dot(merged.astype(BF16), wout_ref[...])
        yield
        e = _dot(p_ref[0, rs, :].astype(BF16), wpp_ref[...])
        gate = _sigmoid(_dot((_rms_scale(x1) * ple_g).astype(BF16), wpg_ref[...]))
        x2 = x1 + gate * e
        if final:
            x2 = _rms_scale(x2) * fin_g
        out_ref[0, rs, :] = x2
        yield

    n_quarters = 4
    blocks = sub // rows
    pending = None
    for r0 in range(0, tm, sub):
        x, h = head(r0)
        b0 = r0 // rows
        for q in range(n_quarters):
            if pending is not None:
                next(pending)
            conv_blocks(b0 + blocks * q // n_quarters, b0 + blocks * (q + 1) // n_quarters)
        pending = tail(r0, x, h)
    for _ in range(n_quarters):
        next(pending)
    cbuf_ref[0:hist, :] = cbuf_ref[tm * CF_SLABS:tm * CF_SLABS + hist, :]


def _mix_out(x, hm, o, p, wz, w5, wdn, wcf, wout, wpg, wpp, vecs, dww, dwb, *, tm, final):
    b, t, d = x.shape
    assert wcf.shape[0] == CF_SLABS * LANES
    rows = CF_TIME_BLOCK
    kernel = functools.partial(_mix_out_kernel, tm=tm, sub=min(tm, MIX_SUB_TILE), rows=rows, final=final)
    tile = lambda w: pl.BlockSpec((1, tm, w), lambda i, j: (i, j, 0))
    return pl.pallas_call(
        kernel,
        grid=(b, t // tm),
        in_specs=[
            tile(d), tile(d), tile(o.shape[-1]), tile(p.shape[-1]),
            _const_spec(wz.shape), _const_spec(w5.shape), _const_spec(wdn.shape), _const_spec(wcf.shape),
            _const_spec(wout.shape), _const_spec(wpg.shape), _const_spec(wpp.shape),
            _const_spec(vecs.shape), _const_spec(dww.shape), _const_spec(dwb.shape),
        ],
        out_specs=tile(d),
        out_shape=jax.ShapeDtypeStruct((b, t, d), x.dtype),
        scratch_shapes=[
            pltpu.VMEM(((tm + CF_HIST) * CF_SLABS, LANES), F32),
            pltpu.VMEM((tm * CF_SLABS, LANES), F32),
        ],
        compiler_params=pltpu.CompilerParams(
            dimension_semantics=("arbitrary", "arbitrary"), vmem_limit_bytes=VMEM_LIMIT),
        name="mix_out",
    )(x, hm, o, p, wz, w5, wdn, wcf, wout, wpg, wpp, vecs, dww, dwb)


def _pad_rows(a, n):
    return jnp.pad(a, ((0, n - a.shape[0]), (0, 0)))


def _forward(x, p, mix_norm_g, w_in, dn_conv_w, dn_a_log, dn_dt_bias, dn_out_norm_g, w_dn_out,
             cf_dw_w, cf_dw_b, cf_ln_g, cf_ln_b, w_cf_out, w_out, ple_norm_g, w_ple_gate,
             w_ple_proj, final_norm_g, *, tm1, tb, tm3):
    depth, d, _ = w_in.shape
    cf = w_cf_out.shape[1]
    qkv_w = 3 * DN_WIDTH
    o_z, o_bd = qkv_w, qkv_w + DN_WIDTH
    o_glu = o_bd + 2 * N_HEADS
    row = lambda v: v.reshape(1, -1).astype(F32)
    for i in range(depth):
        wi = w_in[i]
        wqkv = wi[:, :qkv_w].astype(BF16)
        wbd = jnp.pad(wi[:, o_bd:o_glu], ((0, 0), (0, LANES - 2 * N_HEADS))).astype(BF16)
        cw = dn_conv_w[i].astype(F32).reshape(DN_CONV, 3, N_HEADS, HEAD_DIM).transpose(1, 0, 2, 3)
        pad_heads = lambda v: jnp.pad(v.astype(F32), (N_HEADS, LANES - 2 * N_HEADS)).reshape(1, LANES)
        bdp = _pad_rows(jnp.concatenate([pad_heads(dn_a_log[i]), pad_heads(dn_dt_bias[i])]), SUBLANES)
        assert cf == d
        wz = wi[:, o_z:o_bd].astype(BF16)
        w5 = wi[:, o_glu:].astype(BF16)
        vecs = _pad_rows(jnp.concatenate([
            row(cf_ln_g[i]), row(cf_ln_b[i]), row(ple_norm_g[i]), row(final_norm_g)]), SUBLANES)
        dww = _pad_rows(cf_dw_w[i].astype(F32), CF_HIST).reshape(CF_HIST * CF_SLABS, LANES)
        dwb = cf_dw_b[i].astype(F32).reshape(CF_SLABS, LANES)

        qkv, bg, hm = _dn_proj(x, row(mix_norm_g[i]), wqkv, wbd, cw, bdp, tm=tm1)
        o = _dn_delta(qkv, bg, row(dn_out_norm_g[i]), tb=tb, nb=DELTA_ROWS if x.shape[0] % DELTA_ROWS == 0 else 1)
        x = _mix_out(x, hm, o, p[i], wz, w5, w_dn_out[i].astype(BF16), w_cf_out[i].astype(BF16),
                     w_out[i].astype(BF16), w_ple_gate[i].astype(BF16), w_ple_proj[i].astype(BF16),
                     vecs, dww, dwb, tm=tm3, final=(i == depth - 1))
    return x


def kernel(x, p, mix_norm_g, w_in, dn_conv_w, dn_a_log, dn_dt_bias, dn_out_norm_g, w_dn_out,
           cf_dw_w, cf_dw_b, cf_ln_g, cf_ln_b, w_cf_out, w_out, ple_norm_g, w_ple_gate,
           w_ple_proj, final_norm_g):
    t = x.shape[1]
    return _forward(x, p, mix_norm_g, w_in, dn_conv_w, dn_a_log, dn_dt_bias, dn_out_norm_g,
                    w_dn_out, cf_dw_w, cf_dw_b, cf_ln_g, cf_ln_b, w_cf_out, w_out, ple_norm_g,
                    w_ple_gate, w_ple_proj, final_norm_g,
                    tm1=min(t, TIME_TILE), tb=min(t, TIME_TILE), tm3=min(t, TIME_TILE))
```
